```python
import math
import jax
import jax.numpy as jnp
from jax import lax
import numpy as np

D_MODEL = 1024
BATCH = 4
SEQ = 8192
DEPTH = 2

BLOCK_Q = 128
EPS = 1e-6
NEG = -1e30
D_FF = 2816

MLA_HEADS = 8
MLA_Q_RANK = 256
MLA_KV_RANK = 128
MLA_NOPE = 64
MLA_ROPE = 32
MLA_V = 64
ROPE_THETA = 10000.0

SWA_HEADS = 8
SWA_KV_HEADS = 2
SWA_HD = 64
SWA_WINDOW = 128

NSA_HEADS = 8
NSA_KV_HEADS = 2
NSA_HD = 64
NSA_CMP_LEN = 32
NSA_CMP_STRIDE = 16
NSA_CMP_HIDDEN = 128
NSA_SEL_LEN = 64
NSA_TOPK = 16
NSA_WINDOW = 512
NSA_FORCE = 1e4

DIFF_HEADS = 4
DIFF_HD = 64

N_BUCKETS = 32
MAX_DIST = 128
N_BIAS_HEADS = SWA_HEADS + NSA_HEADS + DIFF_HEADS

N_BRANCH = 4
BRANCH_W = 512

COL_SIZES = (
    MLA_Q_RANK, MLA_KV_RANK, MLA_ROPE,
    SWA_HEADS * SWA_HD, SWA_KV_HEADS * SWA_HD, SWA_KV_HEADS * SWA_HD,
    NSA_HEADS * NSA_HD,
    NSA_KV_HEADS * NSA_HD, NSA_KV_HEADS * NSA_HD,
    NSA_KV_HEADS * NSA_HD, NSA_KV_HEADS * NSA_HD,
    NSA_KV_HEADS * NSA_HD, NSA_KV_HEADS * NSA_HD,
    3 * NSA_HEADS,
    DIFF_HEADS * 2 * DIFF_HD, DIFF_HEADS * 2 * DIFF_HD, DIFF_HEADS * 2 * DIFF_HD,
)
IN_COLS = sum(COL_SIZES)

kernel_name = 'hybrid_mla_swa_nsa_diff_block'


def rmsnorm(x, g):
    xf = x.astype(jnp.float32)
    y = xf * lax.rsqrt(jnp.mean(xf * xf, -1, keepdims=True) + EPS)
    return (y * g.astype(jnp.float32)).astype(x.dtype)


def swiglu(x, w_gate, w_up, w_down):
    return (jax.nn.silu(x @ w_gate) * (x @ w_up)) @ w_down


def rope(x, pos):
    half = x.shape[-1] // 2
    freqs = ROPE_THETA ** (-jnp.arange(half, dtype=jnp.float32) / half)
    ang = pos[:, None].astype(jnp.float32) * freqs
    cos, sin = jnp.cos(ang), jnp.sin(ang)
    xf = x.astype(jnp.float32)
    x1, x2 = xf[..., :half], xf[..., half:]
    return jnp.concatenate([x1 * cos - x2 * sin, x1 * sin + x2 * cos], -1).astype(x.dtype)


def t5_bucket(dist):
    n = jnp.maximum(dist, 0)
    max_exact = N_BUCKETS // 2
    nf = jnp.maximum(n, 1).astype(jnp.float32)
    large = max_exact + (jnp.log(nf / max_exact) / math.log(MAX_DIST / max_exact)
                         * (N_BUCKETS - max_exact)).astype(jnp.int32)
    large = jnp.minimum(large, N_BUCKETS - 1)
    return jnp.where(n < max_exact, n, large)


def rel_bias(table, q_pos, k_pos):
    b = t5_bucket(q_pos[:, None] - k_pos[None, :])
    return jnp.moveaxis(table[b], -1, 0).astype(jnp.float32)


def masked_softmax(s, mask):
    s = jnp.where(mask, s, NEG)
    m = jnp.max(s, -1, keepdims=True)
    e = jnp.where(mask, jnp.exp(s - m), 0.0)
    den = jnp.sum(e, -1, keepdims=True)
    return e / jnp.where(den > 0, den, 1.0)


def dense_causal_attn(q, k, v, scale):
    B, H, S, _ = q.shape
    dv = v.shape[-1]
    kpos = jnp.arange(S)

    def blk(i):
        start = i * BLOCK_Q
        qpos = start + jnp.arange(BLOCK_Q)
        qb = lax.dynamic_slice_in_dim(q, start, BLOCK_Q, axis=2)
        s = jnp.einsum('bhqd,bhkd->bhqk', qb, k).astype(jnp.float32) * scale
        s = jnp.where(kpos[None, :] <= qpos[:, None], s, NEG)
        p = jax.nn.softmax(s, -1)
        return jnp.einsum('bhqk,bhkd->bhqd', p.astype(v.dtype), v)

    o = lax.map(blk, jnp.arange(S // BLOCK_Q))
    return o.transpose(1, 0, 3, 2, 4).reshape(B, S, H * dv)


def banded_attn(q, k, v, window, table, sinks):
    B, Hkv, G, S, d = q.shape
    dv = v.shape[-1]
    nprev = -(-window // BLOCK_Q)
    pad = nprev * BLOCK_Q
    span = pad + BLOCK_Q
    kp = jnp.pad(k, ((0, 0), (0, 0), (pad, 0), (0, 0)))
    vp = jnp.pad(v, ((0, 0), (0, 0), (pad, 0), (0, 0)))
    scale = d ** -0.5

    def blk(i):
        start = i * BLOCK_Q
        qpos = start + jnp.arange(BLOCK_Q)
        kpos = start - pad + jnp.arange(span)
        qb = lax.dynamic_slice_in_dim(q, start, BLOCK_Q, axis=3)
        kb = lax.dynamic_slice_in_dim(kp, start, span, axis=2)
        vb = lax.dynamic_slice_in_dim(vp, start, span, axis=2)
        s = jnp.einsum('bhgqd,bhkd->bhgqk', qb, kb).astype(jnp.float32) * scale
        s = s + rel_bias(table, qpos, kpos).reshape(Hkv, G, BLOCK_Q, span)
        dist = qpos[:, None] - kpos[None, :]
        mask = (dist >= 0) & (dist < window) & (kpos[None, :] >= 0)
        s = jnp.where(mask, s, NEG)
        if sinks is None:
            p = jax.nn.softmax(s, -1)
        else:
            sk = sinks.astype(jnp.float32).reshape(Hkv, G, 1, 1)
            m = jnp.maximum(jnp.max(s, -1, keepdims=True), sk)
            e = jnp.exp(s - m)
            p = e / (jnp.sum(e, -1, keepdims=True) + jnp.exp(sk - m))
        return jnp.einsum('bhgqk,bhkd->bhgqd', p.astype(vb.dtype), vb)

    o = lax.map(blk, jnp.arange(S // BLOCK_Q))
    return jnp.moveaxis(o, 0, 3).reshape(B, Hkv, G, S, dv)


def mla_mixer(cq, ckv, k_rope, q_norm, kv_norm, w_uq, w_ukv, pos):
    B, S, _ = cq.shape
    H = MLA_HEADS
    q = (rmsnorm(cq, q_norm) @ w_uq).reshape(B, S, H, MLA_NOPE + MLA_ROPE).transpose(0, 2, 1, 3)
    q = jnp.concatenate([q[..., :MLA_NOPE], rope(q[..., MLA_NOPE:], pos)], -1)
    kv = (rmsnorm(ckv, kv_norm) @ w_ukv).reshape(B, S, H, MLA_NOPE + MLA_V).transpose(0, 2, 1, 3)
    k_r = jnp.broadcast_to(rope(k_rope[:, None], pos), (B, H, S, MLA_ROPE))
    k = jnp.concatenate([kv[..., :MLA_NOPE], k_r], -1)
    v = kv[..., MLA_NOPE:]
    return dense_causal_attn(q, k, v, (MLA_NOPE + MLA_ROPE) ** -0.5)


def swa_mixer(q, k, v, sinks, table):
    B, S, _ = q.shape
    G = SWA_HEADS // SWA_KV_HEADS
    q = q.reshape(B, S, SWA_KV_HEADS, G, SWA_HD).transpose(0, 2, 3, 1, 4)
    k = k.reshape(B, S, SWA_KV_HEADS, SWA_HD).transpose(0, 2, 1, 3)
    v = v.reshape(B, S, SWA_KV_HEADS, SWA_HD).transpose(0, 2, 1, 3)
    o = banded_attn(q, k, v, SWA_WINDOW, table, sinks)
    return o.transpose(0, 3, 1, 2, 4).reshape(B, S, SWA_HEADS * SWA_HD)


def nsa_compress(t, pos_emb, w1, w2):
    B, Hkv, S, d = t.shape
    nc = (S - NSA_CMP_LEN) // NSA_CMP_STRIDE + 1
    idx = np.arange(nc)[:, None] * NSA_CMP_STRIDE + np.arange(NSA_CMP_LEN)[None, :]
    blocks = jnp.take(t, jnp.asarray(idx, dtype=jnp.int32), axis=2) + pos_emb
    hdn = jax.nn.gelu(blocks.reshape(B, Hkv, nc, NSA_CMP_LEN * d) @ w1)
    return hdn @ w2


def nsa_mixer(q, kc, vc, ks, vs, kw, vw, gate_logits, cmp_pos, cmp_w1, cmp_w2, table):
    B, S, _ = q.shape
    Hkv, G, d = NSA_KV_HEADS, NSA_HEADS // NSA_KV_HEADS, NSA_HD
    SEL = NSA_SEL_LEN
    q = q.reshape(B, S, Hkv, G, d).transpose(0, 2, 3, 1, 4)

    def heads(t):
        return t.reshape(B, S, Hkv, d).transpose(0, 2, 1, 3)

    kc, vc, ks, vs, kw, vw = heads(kc), heads(vc), heads(ks), heads(vs), heads(kw), heads(vw)
    scale = d ** -0.5
    kcmp = nsa_compress(kc, cmp_pos[0], cmp_w1[0], cmp_w2[0])
    vcmp = nsa_compress(vc, cmp_pos[1], cmp_w1[1], cmp_w2[1])
    nc = kcmp.shape[2]
    c_start = np.arange(nc) * NSA_CMP_STRIDE
    c_last = jnp.asarray(c_start + NSA_CMP_LEN - 1, dtype=jnp.int32)
    nsel = S // SEL
    s_start = np.arange(nsel) * SEL
    overlap = jnp.asarray(((c_start[:, None] < s_start[None, :] + SEL)
                           & (s_start[None, :] < c_start[:, None] + NSA_CMP_LEN)).astype(np.float32))
    topk = min(NSA_TOPK, nsel)
    ksb = ks.reshape(B, Hkv, nsel, SEL, d)
    vsb = vs.reshape(B, Hkv, nsel, SEL, d)
    tab = table.reshape(N_BUCKETS, Hkv, G)
    gather = jax.vmap(jax.vmap(lambda blocks, ix: blocks[ix]))
    group_bias = jax.vmap(lambda bk, tb: tb[bk], in_axes=(1, 1), out_axes=1)
    sel_ids = jnp.arange(nsel)

    def blk(i):
        start = i * BLOCK_Q
        qpos = start + jnp.arange(BLOCK_Q)
        qb = lax.dynamic_slice_in_dim(q, start, BLOCK_Q, axis=3)
        s_c = jnp.einsum('bhgqd,bhcd->bhgqc', qb, kcmp).astype(jnp.float32) * scale
        p_c = masked_softmax(s_c, c_last[None, :] <= qpos[:, None])
        o_c = jnp.einsum('bhgqc,bhcd->bhgqd', p_c.astype(vcmp.dtype), vcmp)
        imp = jnp.einsum('bhgqc,cn->bhqn', p_c, overlap)
        cur = qpos // SEL
        causal = sel_ids[None, :] * SEL <= qpos[:, None]
        forced = ((sel_ids[None, :] == 0) | (sel_ids[None, :] == cur[:, None])
                  | (sel_ids[None, :] == cur[:, None] - 1))
        imp = jnp.where(causal, jnp.where(forced, NSA_FORCE, imp), -1.0)
        _, top = lax.top_k(imp, topk)
        k_sel = gather(ksb, top).reshape(B, Hkv, BLOCK_Q, topk * SEL, d)
        v_sel = gather(vsb, top).reshape(B, Hkv, BLOCK_Q, topk * SEL, d)
        kpos = (top[..., None] * SEL + jnp.arange(SEL)).reshape(B, Hkv, BLOCK_Q, topk * SEL)
        dist = qpos[:, None] - kpos
        bias = jnp.moveaxis(group_bias(t5_bucket(dist), tab), -1, 2).astype(jnp.float32)
        s_s = jnp.einsum('bhgqd,bhqkd->bhgqk', qb, k_sel).astype(jnp.float32) * scale + bias
        s_s = jnp.where((dist >= 0)[:, :, None], s_s, NEG)
        p_s = jax.nn.softmax(s_s, -1)
        o_s = jnp.einsum('bhgqk,bhqkd->bhgqd', p_s.astype(v_sel.dtype), v_sel)
        return o_c, o_s

    o_c, o_s = lax.map(blk, jnp.arange(S // BLOCK_Q))
    o_c = jnp.moveaxis(o_c, 0, 3).reshape(B, Hkv, G, S, d)
    o_s = jnp.moveaxis(o_s, 0, 3).reshape(B, Hkv, G, S, d)
    o_w = banded_attn(q, kw, vw, NSA_WINDOW, table, None)
    g = jax.nn.sigmoid(gate_logits.astype(jnp.float32)).reshape(B, S, Hkv, G, 3).transpose(0, 2, 3, 1, 4)
    o = g[..., 0:1] * o_c + g[..., 1:2] * o_s + g[..., 2:3] * o_w
    return o.astype(q.dtype).transpose(0, 3, 1, 2, 4).reshape(B, S, NSA_HEADS * d)


def diff_mixer(q, k, v, lam_params, subln, table, layer):
    B, S, _ = q.shape
    H, d = DIFF_HEADS, DIFF_HD
    q = q.reshape(B, S, H, 2, d).transpose(0, 2, 3, 1, 4)
    k = k.reshape(B, S, H, 2, d).transpose(0, 2, 3, 1, 4)
    v = v.reshape(B, S, H, 2 * d).transpose(0, 2, 1, 3)
    lam_init = 0.8 - 0.6 * math.exp(-0.3 * layer)
    lp = lam_params.astype(jnp.float32)
    lam = jnp.exp(jnp.sum(lp[0] * lp[1])) - jnp.exp(jnp.sum(lp[2] * lp[3])) + lam_init
    scale = d ** -0.5
    kpos = jnp.arange(S)

    def blk(i):
        start = i * BLOCK_Q
        qpos = start + jnp.arange(BLOCK_Q)
        qb = lax.dynamic_slice_in_dim(q, start, BLOCK_Q, axis=3)
        s = jnp.einsum('bhmqd,bhmkd->bhmqk', qb, k).astype(jnp.float32) * scale
        s = s + rel_bias(table, qpos, kpos)[:, None]
        s = jnp.where(kpos[None, :] <= qpos[:, None], s, NEG)
        p = jax.nn.softmax(s, -1)
        a = p[:, :, 0] - lam * p[:, :, 1]
        return jnp.einsum('bhqk,bhkd->bhqd', a.astype(v.dtype), v)

    o = lax.map(blk, jnp.arange(S // BLOCK_Q))
    o = jnp.moveaxis(o, 0, 2).reshape(B, H, S, 2 * d)
    o = rmsnorm(o, subln) * (1.0 - lam_init)
    return o.transpose(0, 2, 1, 3).reshape(B, S, H * 2 * d)


def setup_inputs(seed: int = 0) -> dict:
    key = jax.random.key(seed)
    k = jax.random.split(key, 22)
    f32 = jnp.float32

    def w(kk, shape, fan_in):
        return jax.random.normal(kk, shape, f32) * (fan_in ** -0.5)

    def gain(kk, shape):
        return 1.0 + 0.05 * jax.random.normal(kk, shape, f32)

    L, d = NSA_CMP_LEN, NSA_HD
    return {
        'x': jax.random.normal(k[0], (BATCH, SEQ, D_MODEL), f32),
        'norm_g': gain(k[1], (DEPTH, 3, D_MODEL)),
        'w_in': w(k[2], (DEPTH, D_MODEL, IN_COLS), D_MODEL),
        'mla_q_norm': gain(k[3], (DEPTH, MLA_Q_RANK)),
        'mla_kv_norm': gain(k[4], (DEPTH, MLA_KV_RANK)),
        'mla_w_uq': w(k[5], (DEPTH, MLA_Q_RANK, MLA_HEADS * (MLA_NOPE + MLA_ROPE)), MLA_Q_RANK),
        'mla_w_ukv': w(k[6], (DEPTH, MLA_KV_RANK, MLA_HEADS * (MLA_NOPE + MLA_V)), MLA_KV_RANK),
        'swa_sinks': 0.5 * jax.random.normal(k[7], (DEPTH, SWA_HEADS), f32),
        'nsa_cmp_pos': 0.1 * jax.random.normal(k[8], (DEPTH, 2, L, d), f32),
        'nsa_cmp_w1': w(k[9], (DEPTH, 2, L * d, NSA_CMP_HIDDEN), L * d),
        'nsa_cmp_w2': w(k[10], (DEPTH, 2, NSA_CMP_HIDDEN, d), NSA_CMP_HIDDEN),
        'diff_lambda': 0.1 * jax.random.normal(k[11], (DEPTH, 4, DIFF_HD), f32),
        'diff_subln': gain(k[12], (DEPTH, 2 * DIFF_HD)),
        'rel_bias_table': 0.2 * jax.random.normal(k[13], (N_BUCKETS, N_BIAS_HEADS), f32),
        'w_branch': w(k[14], (DEPTH, N_BRANCH, BRANCH_W, D_MODEL), BRANCH_W),
        'w_gate': w(k[15], (DEPTH, N_BRANCH, D_MODEL, D_MODEL), D_MODEL),
        'w_o': w(k[16], (DEPTH, D_MODEL, D_MODEL), D_MODEL),
        'ffn_w_gate': w(k[17], (DEPTH, 2, D_MODEL, D_FF), D_MODEL),
        'ffn_w_up': w(k[18], (DEPTH, 2, D_MODEL, D_FF), D_MODEL),
        'ffn_w_down': w(k[19], (DEPTH, 2, D_FF, D_MODEL), D_FF),
        'final_g': gain(k[20], (D_MODEL,)),
    }


def reference(x, norm_g, w_in, mla_q_norm, mla_kv_norm, mla_w_uq, mla_w_ukv, swa_sinks,
              nsa_cmp_pos, nsa_cmp_w1, nsa_cmp_w2, diff_lambda, diff_subln, rel_bias_table,
              w_branch, w_gate, w_o, ffn_w_gate, ffn_w_up, ffn_w_down, final_g):
    S = x.shape[1]
    pos = jnp.arange(S, dtype=jnp.int32)
    tab_swa = rel_bias_table[:, :SWA_HEADS]
    tab_nsa = rel_bias_table[:, SWA_HEADS:SWA_HEADS + NSA_HEADS]
    tab_diff = rel_bias_table[:, SWA_HEADS + NSA_HEADS:]
    offs = [int(o) for o in np.cumsum(COL_SIZES)[:-1]]
    h = x
    for l in range(DEPTH):
        h = h + 0.5 * swiglu(rmsnorm(h, norm_g[l, 0]), ffn_w_gate[l, 0], ffn_w_up[l, 0], ffn_w_down[l, 0])
        u = rmsnorm(h, norm_g[l, 1])
        cols = u @ w_in[l]
        (cq, ckv, krope, sq, sk, sv, nq, nkc, nvc, nks, nvs, nkw, nvw, ngate,
         dq, dk, dv) = jnp.split(cols, offs, axis=-1)
        y_a = mla_mixer(cq, ckv, krope, mla_q_norm[l], mla_kv_norm[l], mla_w_uq[l], mla_w_ukv[l], pos)
        y_b = swa_mixer(sq, sk, sv, swa_sinks[l], tab_swa)
        y_c = nsa_mixer(nq, nkc, nvc, nks, nvs, nkw, nvw, ngate,
                        nsa_cmp_pos[l], nsa_cmp_w1[l], nsa_cmp_w2[l], tab_nsa)
        y_d = diff_mixer(dq, dk, dv, diff_lambda[l], diff_subln[l], tab_diff, l)
        merged = jax.nn.sigmoid(u @ w_gate[l, 0]) * (y_a @ w_branch[l, 0])
        merged = merged + jax.nn.sigmoid(u @ w_gate[l, 1]) * (y_b @ w_branch[l, 1])
        merged = merged + jax.nn.sigmoid(u @ w_gate[l, 2]) * (y_c @ w_branch[l, 2])
        merged = merged + jax.nn.sigmoid(u @ w_gate[l, 3]) * (y_d @ w_branch[l, 3])
        h = h + merged @ w_o[l]
        h = h + 0.5 * swiglu(rmsnorm(h, norm_g[l, 2]), ffn_w_gate[l, 1], ffn_w_up[l, 1], ffn_w_down[l, 1])
    return rmsnorm(h, final_g)
```

```python
import functools
import math

import numpy as np
import jax
import jax.numpy as jnp
from jax import lax
from jax.experimental import pallas as pl
from jax.experimental.pallas import tpu as pltpu

F32 = jnp.float32
BF16 = jnp.bfloat16

D_MODEL = 1024
EPS = 1e-6
NEG = -1e30
D_FF = 2816

MLA_HEADS = 8
MLA_Q_RANK = 256
MLA_KV_RANK = 128
MLA_NOPE = 64
MLA_ROPE = 32
MLA_V = 64
ROPE_THETA = 10000.0

SWA_HEADS = 8
SWA_WINDOW = 128

NSA_HEADS = 8
NSA_HD = 64
NSA_CMP_LEN = 32
NSA_CMP_STRIDE = 16
NSA_CMP_HIDDEN = 128
NSA_SEL_LEN = 64
NSA_TOPK = 16
NSA_WINDOW = 512
NSA_FORCE = 1e4

DIFF_HEADS = 4
DIFF_HD = 64

N_BUCKETS = 32
MAX_DIST = 128

LANE = 128
HALF = 64
VMEM_LIMIT = 56 * 1024 * 1024

O_CQ, O_CKV, O_KROPE, O_SQ, O_SK, O_SV, O_NQ = 0, 256, 384, 416, 928, 1056, 1184
O_NKC, O_NVC, O_NKS, O_NVS, O_NKW, O_NVW, O_NGATE = 1696, 1824, 1952, 2080, 2208, 2336, 2464
O_DQ, O_DK, O_DV = 2488, 3000, 3512

F_COLS = 1024
FB_CKV, FB_KROPE, FB_KSWAP, FB_NKC, FB_NVC, FB_NGATE = 2, 3, 4, 5, 6, 7
G_COLS = 3328
GB_SQ, GB_NQ, GB_DQ, GB_DK, GB_DV = 0, 4, 8, 12, 16
GB_SK, GB_SV, GB_NKS, GB_NVS, GB_NKW, GB_NVW = 20, 21, 22, 23, 24, 25


def _cparams(sem):
    return pltpu.CompilerParams(dimension_semantics=sem, vmem_limit_bytes=VMEM_LIMIT)


def _rms(x, g):
    return x * lax.rsqrt(jnp.mean(x * x, -1, keepdims=True) + EPS) * g


def _dot(a, b):
    return jnp.dot(a, b, preferred_element_type=F32)


def _dot_nt(a, b):
    return lax.dot_general(a, b, (((1,), (1,)), ((), ())), preferred_element_type=F32)


def _split3(x):
    hi = x.astype(BF16)
    r = x - hi.astype(F32)
    mid = r.astype(BF16)
    lo = (r - mid.astype(F32)).astype(BF16)
    return hi, mid, lo


def _dot_exact(x, w01):
    hi, mid, lo = _split3(x)
    return _dot(hi, w01) + _dot(mid, w01) + _dot(lo, w01)


def _ffn_kernel(h_ref, g_ref, wg_ref, wu_ref, wd_ref, gf_ref, o_ref, n_scr, acc_scr, *, final):
    j = pl.program_id(1)

    @pl.when(j == 0)
    def _():
        n_scr[...] = _rms(h_ref[...], g_ref[...]).astype(BF16)
        acc_scr[...] = jnp.zeros_like(acc_scr)

    n = n_scr[...]
    a = _dot(n, wg_ref[...])
    b = _dot(n, wu_ref[...])
    t = (a * jax.nn.sigmoid(a) * b).astype(BF16)
    acc_scr[...] += _dot(t, wd_ref[...])

    @pl.when(j == pl.num_programs(1) - 1)
    def _():
        out = h_ref[...] + 0.5 * acc_scr[...]
        if final:
            out = _rms(out, gf_ref[...])
        o_ref[...] = out


def _ffn(h, g, wg, wu, wd, gf, final):
    T = h.shape[0]
    tm, tf = 512, D_FF // 2
    return pl.pallas_call(
        functools.partial(_ffn_kernel, final=final),
        grid=(T // tm, D_FF // tf),
        in_specs=[
            pl.BlockSpec((tm, D_MODEL), lambda i, j: (i, 0)),
            pl.BlockSpec((1, D_MODEL), lambda i, j: (0, 0)),
            pl.BlockSpec((D_MODEL, tf), lambda i, j: (0, j)),
            pl.BlockSpec((D_MODEL, tf), lambda i, j: (0, j)),
            pl.BlockSpec((tf, D_MODEL), lambda i, j: (j, 0)),
            pl.BlockSpec((1, D_MODEL), lambda i, j: (0, 0)),
        ],
        out_specs=pl.BlockSpec((tm, D_MODEL), lambda i, j: (i, 0)),
        out_shape=jax.ShapeDtypeStruct((T, D_MODEL), F32),
        scratch_shapes=[pltpu.VMEM((tm, D_MODEL), BF16), pltpu.VMEM((tm, D_MODEL), F32)],
        compiler_params=_cparams(("parallel", "arbitrary")),
        name="ffn_half_step",
    )(h, g, wg, wu, wd, gf)


def _proj_kernel(h_ref, g_ref, w_ref, f_ref, gq_ref):
    u = _rms(h_ref[...], g_ref[...]).astype(BF16)
    f_ref[...] = _dot(u, w_ref[:, :F_COLS])
    step = 512
    for c0 in range(0, G_COLS, step):
        c1 = min(c0 + step, G_COLS)
        gq_ref[:, c0:c1] = _dot(u, w_ref[:, F_COLS + c0:F_COLS + c1]).astype(BF16)


def _proj(h, g, wp):
    T = h.shape[0]
    tm = 512
    return pl.pallas_call(
        _proj_kernel,
        grid=(T // tm,),
        in_specs=[
            pl.BlockSpec((tm, D_MODEL), lambda i: (i, 0)),
            pl.BlockSpec((1, D_MODEL), lambda i: (0, 0)),
            pl.BlockSpec((D_MODEL, F_COLS + G_COLS), lambda i: (0, 0)),
        ],
        out_specs=[
            pl.BlockSpec((tm, F_COLS), lambda i: (i, 0)),
            pl.BlockSpec((tm, G_COLS), lambda i: (i, 0)),
        ],
        out_shape=[jax.ShapeDtypeStruct((T, F_COLS), F32),
                   jax.ShapeDtypeStruct((T, G_COLS), BF16)],
        compiler_params=_cparams(("parallel",)),
        name="in_proj",
    )(h, g, wp)


def _proj_weight(w_in):
    ncol = F_COLS + G_COLS
    src = np.zeros((ncol,), np.int32)
    scale = np.zeros((ncol,), np.float32)

    def put(dst, srcs, s=1.0):
        srcs = np.asarray(srcs)
        src[dst:dst + len(srcs)] = srcs
        scale[dst:dst + len(srcs)] = s

    half = MLA_ROPE // 2
    put(0, O_CQ + np.arange(MLA_Q_RANK))
    put(FB_CKV * LANE, O_CKV + np.arange(MLA_KV_RANK))
    put(FB_KROPE * LANE + MLA_NOPE, O_KROPE + np.arange(MLA_ROPE))
    put(FB_KSWAP * LANE + MLA_NOPE, O_KROPE + np.concatenate([np.arange(half, MLA_ROPE), np.arange(half)]))
    put(FB_NKC * LANE, O_NKC + np.arange(LANE))
    put(FB_NVC * LANE, O_NVC + np.arange(LANE))
    put(FB_NGATE * LANE, O_NGATE + np.arange(3 * NSA_HEADS))
    g0 = F_COLS
    pair_perm = np.concatenate([np.concatenate([(j) * HALF + np.arange(HALF), (4 + j) * HALF + np.arange(HALF)])
                                for j in range(4)])
    put(g0 + GB_SQ * LANE, O_SQ + pair_perm, 0.125)
    put(g0 + GB_NQ * LANE, O_NQ + pair_perm, 0.125)
    put(g0 + GB_DQ * LANE, O_DQ + np.arange(512), 0.125)
    put(g0 + GB_DK * LANE, O_DK + np.arange(512))
    put(g0 + GB_DV * LANE, O_DV + np.arange(512))
    for blk, off in ((GB_SK, O_SK), (GB_SV, O_SV), (GB_NKS, O_NKS), (GB_NVS, O_NVS),
                     (GB_NKW, O_NKW), (GB_NVW, O_NVW)):
        put(g0 + blk * LANE, off + np.arange(LANE))
    return (w_in[:, src] * scale[None, :]).astype(BF16)


PAIR_HEAD = np.array([[j, 4 + j] for j in range(4)])


def _mla_prep_kernel(cq_ref, ckv_ref, kr_ref, ks_ref, ct_ref, st_ref, qg_ref, kg_ref,
                     wa_ref, wb_ref, wk_ref, wv_ref, q_ref, k_ref, v_ref, *, scale):
    qn = _rms(cq_ref[0], qg_ref[...]).astype(BF16)
    kn = _rms(ckv_ref[0], kg_ref[...]).astype(BF16)
    ct = ct_ref[...]
    st = st_ref[...]
    krot = kr_ref[0] * ct + ks_ref[0] * st
    v_ref[0] = _dot(kn, wv_ref[...]).astype(BF16)
    for h in range(MLA_HEADS):
        sl = slice(h * LANE, (h + 1) * LANE)
        qa = _dot(qn, wa_ref[:, sl])
        qb = _dot(qn, wb_ref[:, sl])
        q_ref[0, :, sl] = ((qa * ct + qb * st) * scale).astype(BF16)
        k_ref[0, :, sl] = (_dot(kn, wk_ref[:, sl]) + krot).astype(BF16)


def _mla_prep(Fa, ctab, stab, qg, kg, wa, wb, wk, wv):
    B, S, _ = Fa.shape
    tm = 512
    HW = MLA_HEADS * LANE
    full = lambda shape: pl.BlockSpec(shape, lambda b, i: (0,) * len(shape))
    return pl.pallas_call(
        functools.partial(_mla_prep_kernel, scale=(MLA_NOPE + MLA_ROPE) ** -0.5),
        grid=(B, S // tm),
        in_specs=[
            pl.BlockSpec((1, tm, MLA_Q_RANK), lambda b, i: (b, i, 0)),
            pl.BlockSpec((1, tm, LANE), lambda b, i: (b, i, FB_CKV)),
            pl.BlockSpec((1, tm, LANE), lambda b, i: (b, i, FB_KROPE)),
            pl.BlockSpec((1, tm, LANE), lambda b, i: (b, i, FB_KSWAP)),
            pl.BlockSpec((tm, LANE), lambda b, i: (i, 0)),
            pl.BlockSpec((tm, LANE), lambda b, i: (i, 0)),
            full((1, MLA_Q_RANK)), full((1, MLA_KV_RANK)),
            full((MLA_Q_RANK, HW)), full((MLA_Q_RANK, HW)),
            full((MLA_KV_RANK, HW)), full((MLA_KV_RANK, MLA_HEADS * MLA_V)),
        ],
        out_specs=[
            pl.BlockSpec((1, tm, HW), lambda b, i: (b, i, 0)),
            pl.BlockSpec((1, tm, HW), lambda b, i: (b, i, 0)),
            pl.BlockSpec((1, tm, MLA_HEADS * MLA_V), lambda b, i: (b, i, 0)),
        ],
        out_shape=[jax.ShapeDtypeStruct((B, S, HW), BF16),
                   jax.ShapeDtypeStruct((B, S, HW), BF16),
                   jax.ShapeDtypeStruct((B, S, MLA_HEADS * MLA_V), BF16)],
        compiler_params=_cparams(("parallel", "parallel")),
        name="mla_prep",
    )(Fa, Fa, Fa, Fa, ctab, stab, qg, kg, wa, wb, wk, wv)


def _mla_weights(w_uq, w_ukv):
    dq = MLA_NOPE + MLA_ROPE
    half = MLA_ROPE // 2
    z = lambda n: jnp.zeros((w_uq.shape[0], n), F32)
    wa, wb, wk, wv = [], [], [], []
    for h in range(MLA_HEADS):
        nope = w_uq[:, h * dq:h * dq + MLA_NOPE]
        r = w_uq[:, h * dq + MLA_NOPE:(h + 1) * dq]
        wa += [nope, r, z(LANE - dq)]
        wb += [z(MLA_NOPE), r[:, half:], r[:, :half], z(LANE - dq)]
        wk += [w_ukv[:, h * LANE:h * LANE + MLA_NOPE], jnp.zeros((w_ukv.shape[0], LANE - MLA_NOPE), F32)]
        wv += [w_ukv[:, h * LANE + MLA_NOPE:(h + 1) * LANE]]
    cat = lambda xs: jnp.concatenate(xs, axis=1).astype(BF16)
    return cat(wa), cat(wb), cat(wk), cat(wv)


def _rope_tables(S):
    half = MLA_ROPE // 2
    freqs = ROPE_THETA ** (-jnp.arange(half, dtype=F32) / half)
    ang = jnp.arange(S, dtype=jnp.int32)[:, None].astype(F32) * freqs
    cos, sin = jnp.cos(ang), jnp.sin(ang)
    ones = jnp.ones((S, MLA_NOPE), F32)
    zeros = jnp.zeros((S, LANE - MLA_NOPE - MLA_ROPE), F32)
    ctab = jnp.concatenate([ones, cos, cos, zeros], 1)
    stab = jnp.concatenate([jnp.zeros((S, MLA_NOPE), F32), -sin, sin, zeros], 1)
    return ctab, stab


def _t5_bucket(dist):
    n = jnp.maximum(dist, 0)
    max_exact = N_BUCKETS // 2
    nf = jnp.maximum(n, 1).astype(F32)
    large = max_exact + (jnp.log(nf / max_exact) / math.log(MAX_DIST / max_exact)
                         * (N_BUCKETS - max_exact)).astype(jnp.int32)
    large = jnp.minimum(large, N_BUCKETS - 1)
    return jnp.where(n < max_exact, n, large)


def _bias_tiles(tab, t):
    assert t >= MAX_DIST
    i = jnp.arange(t, dtype=jnp.int32)[:, None]
    j = jnp.arange(t, dtype=jnp.int32)[None, :]
    tiles = []
    for rel in range(2):
        b = _t5_bucket(rel * t + i - j)
        tiles.append(jnp.moveaxis(tab[b], -1, 0))
    far = jnp.broadcast_to(tab[N_BUCKETS - 1][:, None, None], (tab.shape[1], t, t))
    tiles.append(far)
    return jnp.stack(tiles, 1).astype(F32)


def _flash_kernel(*refs, t, window, shared_qk, has_bias, bias_shared, has_sel, has_sink, diff, out_dtype):
    qt_ref, kt_ref, first_ref, last_ref, rel_ref = refs[:5]
    refs = list(refs[5:])
    if shared_qk:
        q_ref, k_ref = refs[:2]
        refs = refs[2:]
    else:
        qa_ref, qb_ref, ka_ref, kb_ref = refs[:4]
        refs = refs[4:]
    v_ref = refs.pop(0)
    if has_bias:
        ba_ref = refs.pop(0)
        bb_ref = ba_ref if bias_shared else refs.pop(0)
    if has_sel:
        sa_ref, sb_ref, e_ref = refs[:3]
        refs = refs[3:]
    if has_sink:
        sink_ref = refs.pop(0)
    if diff:
        lam_ref, sub_ref = refs[:2]
        refs = refs[2:]
    o_ref, m_scr, l_scr, acc_scr = refs

    step = pl.program_id(2)
    qt = qt_ref[step]
    kt = kt_ref[step]
    lane = lax.broadcasted_iota(jnp.int32, (1, LANE), 1)
    lo = lane < HALF

    @pl.when(first_ref[step] == 1)
    def _():
        for a in range(2):
            if has_sink:
                m_scr[a] = jnp.broadcast_to(sink_ref[0, a:a + 1, 0:1], (t, 1))
                l_scr[a] = jnp.ones((t, 1), F32)
            else:
                m_scr[a] = jnp.full((t, 1), NEG, F32)
                l_scr[a] = jnp.zeros((t, 1), F32)
            acc_scr[a] = jnp.zeros((t, LANE), F32)

    if shared_qk:
        q = q_ref[0]
        zero = jnp.zeros_like(q)
        qs = (jnp.where(lo, q, zero), jnp.where(lo, zero, q))
        ks = (k_ref[0], k_ref[0])
    else:
        qs = (qa_ref[0], qb_ref[0])
        ks = (ka_ref[0], kb_ref[0])
    v = v_ref[0]

    row = lax.broadcasted_iota(jnp.int32, (t, t), 0)
    col = lax.broadcasted_iota(jnp.int32, (t, t), 1)
    dist = (qt - kt) * t + row - col
    valid = dist >= 0
    if window is not None:
        valid = valid & (dist < window)

    for a in range(2):
        s = _dot_nt(qs[a], ks[a])
        if has_bias:
            s = s + (ba_ref if a == 0 else bb_ref)[0, 0, 0]
        va = valid
        if has_sel:
            picked = _dot((sa_ref if a == 0 else sb_ref)[0], e_ref[...])
            va = va & (picked > 0.5)
        s = jnp.where(va, s, NEG)
        m_prev = m_scr[a]
        m_new = jnp.maximum(m_prev, jnp.max(s, -1, keepdims=True))
        alpha = jnp.exp(m_prev - m_new)
        p = jnp.exp(s - m_new)
        l_scr[a] = alpha * l_scr[a] + jnp.sum(p, -1, keepdims=True)
        acc_scr[a] = alpha * acc_scr[a] + _dot(p.astype(BF16), v)
        m_scr[a] = m_new

    @pl.when(last_ref[step] == 1)
    def _():
        o0 = acc_scr[0] / l_scr[0]
        o1 = acc_scr[1] / l_scr[1]
        if diff:
            lam = lam_ref[0:1, 0:1]
            o = o0 - lam * o1
            o = _rms(o, sub_ref[...]) * lam_ref[1:2, 0:1]
        else:
            o = jnp.where(lo, o0, o1)
        o_ref[0] = o.astype(out_dtype)


def _flash(q_arr, k_arr, v_arr, *, npairs, t, q_blk, k_blk, v_blk, window=None, shared_qk=True,
           bias=None, bias_shared=False, sel=None, emat=None, sinks=None, diff=None, out_dtype=BF16):
    B, S, _ = q_arr.shape
    nq = S // t
    nprev = None if window is None else -(-(window - 1) // t)
    steps = []
    for qi in range(nq):
        k0 = 0 if nprev is None else max(0, qi - nprev)
        for ki in range(k0, qi + 1):
            steps.append((qi, ki, int(ki == k0), int(ki == qi), min(qi - ki, 2)))
    tabs = [jnp.asarray(np.array([s[c] for s in steps], np.int32)) for c in range(5)]
    nsteps = len(steps)

    def qmap(off, stride):
        return lambda b, p, s, qt, kt, fr, la, rl: (b, qt[s], off + stride * p)

    def kmap(off, stride):
        return lambda b, p, s, qt, kt, fr, la, rl: (b, kt[s], off + stride * p)

    in_specs, args = [], []
    if shared_qk:
        in_specs += [pl.BlockSpec((1, t, LANE), qmap(*q_blk)), pl.BlockSpec((1, t, LANE), kmap(*k_blk))]
        args += [q_arr, k_arr]
    else:
        in_specs += [pl.BlockSpec((1, t, LANE), qmap(*q_blk[0])), pl.BlockSpec((1, t, LANE), qmap(*q_blk[1])),
                     pl.BlockSpec((1, t, LANE), kmap(*k_blk[0])), pl.BlockSpec((1, t, LANE), kmap(*k_blk[1]))]
        args += [q_arr, q_arr, k_arr, k_arr]
    in_specs.append(pl.BlockSpec((1, t, LANE), kmap(*v_blk)))
    args.append(v_arr)
    if bias is not None:
        for a in range(1 if bias_shared else 2):
            in_specs.append(pl.BlockSpec((1, 1, 1, t, t),
                                         lambda b, p, s, qt, kt, fr, la, rl, a=a: (a, p, rl[s], 0, 0)))
            args.append(bias)
    if sel is not None:
        nselp = emat.shape[0]
        for a in range(2):
            in_specs.append(pl.BlockSpec((1, t, nselp),
                                         lambda b, p, s, qt, kt, fr, la, rl, a=a: (b, qt[s], a)))
            args.append(sel)
        in_specs.append(pl.BlockSpec((nselp, t), lambda b, p, s, qt, kt, fr, la, rl: (0, kt[s])))
        args.append(emat)
    if sinks is not None:
        in_specs.append(pl.BlockSpec((1, 8, LANE), lambda b, p, s, qt, kt, fr, la, rl: (p, 0, 0)))
        args.append(sinks)
    if diff is not None:
        lam_arr, sub = diff
        in_specs.append(pl.BlockSpec((8, LANE), lambda b, p, s, qt, kt, fr, la, rl: (0, 0)))
        in_specs.append(pl.BlockSpec((1, LANE), lambda b, p, s, qt, kt, fr, la, rl: (0, 0)))
        args += [lam_arr, sub]

    kern = functools.partial(
        _flash_kernel, t=t, window=window, shared_qk=shared_qk, has_bias=bias is not None,
        bias_shared=bias_shared, has_sel=sel is not None, has_sink=sinks is not None,
        diff=diff is not None, out_dtype=out_dtype)
    return pl.pallas_call(
        kern,
        grid_spec=pltpu.PrefetchScalarGridSpec(
            num_scalar_prefetch=5,
            grid=(B, npairs, nsteps),
            in_specs=in_specs,
            out_specs=pl.BlockSpec((1, t, LANE), lambda b, p, s, qt, kt, fr, la, rl: (b, qt[s], p)),
            scratch_shapes=[pltpu.VMEM((2, t, 1), F32), pltpu.VMEM((2, t, 1), F32),
                            pltpu.VMEM((2, t, LANE), F32)],
        ),
        out_shape=jax.ShapeDtypeStruct((B, S, npairs * LANE), out_dtype),
        compiler_params=_cparams(("parallel", "parallel", "arbitrary")),
        name="flash_pairs",
    )(*tabs, *args)


def _compress_kernel(r_ref, pt_ref, pb_ref, w1_ref, w2_ref, o_ref):
    r = r_ref[0, 0]
    n = r.shape[0]
    xt = (r + pt_ref[0]).astype(BF16)
    xb = (r + pb_ref[0]).astype(BF16)
    out = jnp.zeros((n, LANE), F32)
    for h in range(2):
        top = _dot(xt, w1_ref[0, h, 0])
        bot = _dot(xb, w1_ref[0, h, 1])
        x = top + pltpu.roll(bot, n - 1, 0)
        hd = 0.5 * x * (1.0 + jnp.tanh(math.sqrt(2.0 / math.pi) * (x + 0.044715 * (x * x * x))))
        out = out + _dot(hd.astype(BF16), w2_ref[0, h])
    o_ref[0, 0] = out.astype(BF16)


def _compress(r2, ptop, pbot, w1e, w2e):
    _, B, n, K = r2.shape
    return pl.pallas_call(
        _compress_kernel,
        grid=(2, B),
        in_specs=[
            pl.BlockSpec((1, 1, n, K), lambda c, b: (c, b, 0, 0)),
            pl.BlockSpec((1, 1, K), lambda c, b: (c, 0, 0)),
            pl.BlockSpec((1, 1, K), lambda c, b: (c, 0, 0)),
            pl.BlockSpec((1, 2, 2, K, NSA_CMP_HIDDEN), lambda c, b: (c, 0, 0, 0, 0)),
            pl.BlockSpec((1, 2, NSA_CMP_HIDDEN, LANE), lambda c, b: (c, 0, 0, 0)),
        ],
        out_specs=pl.BlockSpec((1, 1, n, LANE), lambda c, b: (c, b, 0, 0)),
        out_shape=jax.ShapeDtypeStruct((2, B, n, LANE), BF16),
        compiler_params=_cparams(("parallel", "parallel")),
        name="nsa_compress",
    )(r2, ptop, pbot, w1e, w2e)


def _compress_weights(cmp_pos, cmp_w1, cmp_w2):
    L2 = NSA_CMP_STRIDE
    d = NSA_HD
    w1 = cmp_w1.reshape(2, 2, L2, d, NSA_CMP_HIDDEN)
    z = jnp.zeros_like(w1)
    e0 = jnp.concatenate([w1, z], axis=3)
    e1 = jnp.concatenate([z, w1], axis=3)
    w1e = jnp.stack([e0, e1], axis=1).reshape(2, 2, 2, L2 * 2 * d, NSA_CMP_HIDDEN).astype(BF16)
    pos = cmp_pos.reshape(2, 2, L2, 1, d)
    pos = jnp.broadcast_to(pos, (2, 2, L2, 2, d)).reshape(2, 2, 1, L2 * 2 * d)
    zw = jnp.zeros_like(cmp_w2)
    w2e = jnp.stack([jnp.concatenate([cmp_w2, zw], -1), jnp.concatenate([zw, cmp_w2], -1)], 1).astype(BF16)
    return pos[:, 0], pos[:, 1], w1e, w2e


def _cmp_attn_kernel(q_ref, kc_ref, vc_ref, ov_ref, oc_ref, sel_ref, *, tq, ncp, nselp):
    i = pl.program_id(1)
    qpos = i * tq + lax.broadcasted_iota(jnp.int32, (tq, 1), 0)
    cidx = lax.broadcasted_iota(jnp.int32, (1, ncp), 1)
    cvalid = (cidx * NSA_CMP_STRIDE + (NSA_CMP_LEN - 1)) <= qpos
    lane = lax.broadcasted_iota(jnp.int32, (1, LANE), 1)
    lo = lane < HALF
    kc = kc_ref[0, 0]
    vc = vc_ref[0, 0]
    psum = [jnp.zeros((tq, ncp), F32), jnp.zeros((tq, ncp), F32)]
    for j in range(4):
        q2 = q_ref[0, :, j * LANE:(j + 1) * LANE]
        zero = jnp.zeros_like(q2)
        outs = []
        for a in range(2):
            qa = jnp.where(lo, q2, zero) if a == 0 else jnp.where(lo, zero, q2)
            s = jnp.where(cvalid, _dot_nt(qa, kc), NEG)
            m = jnp.max(s, -1, keepdims=True)
            e = jnp.where(cvalid, jnp.exp(s - m), 0.0)
            den = jnp.sum(e, -1, keepdims=True)
            p = e / jnp.where(den > 0, den, 1.0)
            psum[a] = psum[a] + p
            outs.append(_dot(p.astype(BF16), vc))
        oc_ref[0, :, j * LANE:(j + 1) * LANE] = jnp.where(lo, outs[0], outs[1])

    n = lax.broadcasted_iota(jnp.int32, (1, nselp), 1)
    cur = qpos >> int(math.log2(NSA_SEL_LEN))
    causal = (n * NSA_SEL_LEN) <= qpos
    forced = (n == 0) | (n == cur) | (n == cur - 1)
    nf = n.astype(F32)
    for a in range(2):
        imp = _dot_exact(psum[a], ov_ref[...])
        x = jnp.where(causal, jnp.where(forced, NSA_FORCE, imp), -1.0)
        picked = jnp.zeros((tq, nselp), F32)
        for _ in range(NSA_TOPK):
            mx = jnp.max(x, -1, keepdims=True)
            idx = jnp.min(jnp.where(x == mx, nf, float(nselp)), -1, keepdims=True)
            hit = nf == idx
            picked = jnp.where(hit, 1.0, picked)
            x = jnp.where(hit, -2.0, x)
        sel_ref[0, :, a * nselp:(a + 1) * nselp] = jnp.where(causal, picked, 0.0).astype(BF16)


def _cmp_attn(G3, kv2, ovl):
    B, S, _ = G3.shape
    ncp = kv2.shape[2]
    nselp = ovl.shape[1]
    tq = 256
    return pl.pallas_call(
        functools.partial(_cmp_attn_kernel, tq=tq, ncp=ncp, nselp=nselp),
        grid=(B, S // tq),
        in_specs=[
            pl.BlockSpec((1, tq, 4 * LANE), lambda b, i: (b, i, GB_NQ // 4)),
            pl.BlockSpec((1, 1, ncp, LANE), lambda b, i: (0, b, 0, 0)),
            pl.BlockSpec((1, 1, ncp, LANE), lambda b, i: (1, b, 0, 0)),
            pl.BlockSpec((ncp, nselp), lambda b, i: (0, 0)),
        ],
        out_specs=[
            pl.BlockSpec((1, tq, 4 * LANE), lambda b, i: (b, i, 0)),
            pl.BlockSpec((1, tq, 2 * nselp), lambda b, i: (b, i, 0)),
        ],
        out_shape=[jax.ShapeDtypeStruct((B, S, 4 * LANE), F32),
                   jax.ShapeDtypeStruct((B, S, 2 * nselp), BF16)],
        compiler_params=_cparams(("parallel", "parallel")),
        name="nsa_cmp_attn_topk",
    )(G3, kv2, kv2, ovl)


def _merge_kernel(h_ref, g_ref, ya_ref, yb_ref, oc_ref, os_ref, ow_ref, ng_ref, yd_ref,
                  wg_ref, wb_ref, wo_ref, p3_ref, o_ref):
    h = h_ref[...]
    u = _rms(h, g_ref[...]).astype(BF16)
    gexp = _dot_exact(jax.nn.sigmoid(ng_ref[...]), p3_ref[...])
    W = 4 * LANE
    yc = gexp[:, :W] * oc_ref[...] + gexp[:, W:2 * W] * os_ref[...] + gexp[:, 2 * W:] * ow_ref[...]
    ys = (ya_ref[...], yb_ref[...], yc.astype(BF16), yd_ref[...])
    merged = jnp.zeros(h.shape, F32)
    for i in range(4):
        gate = jax.nn.sigmoid(_dot(u, wg_ref[i]))
        merged = merged + gate * _dot(ys[i], wb_ref[i])
    o_ref[...] = h + _dot(merged.astype(BF16), wo_ref[...])


def _merge(h, g, ya, yb, oc, osel, ow, Fa, yd, wg, wb, wo, p3):
    T = h.shape[0]
    tm = 256
    W = 4 * LANE
    row = lambda w: pl.BlockSpec((tm, w), lambda i: (i, 0))
    const = lambda shape: pl.BlockSpec(shape, lambda i: (0,) * len(shape))
    return pl.pallas_call(
        _merge_kernel,
        grid=(T // tm,),
        in_specs=[
            row(D_MODEL), const((1, D_MODEL)), row(W), row(W), row(W), row(W), row(W),
            pl.BlockSpec((tm, LANE), lambda i: (i, FB_NGATE)), row(W),
            const((4, D_MODEL, D_MODEL)), const((4, W, D_MODEL)), const((D_MODEL, D_MODEL)),
            const((LANE, 3 * W)),
        ],
        out_specs=row(D_MODEL),
        out_shape=jax.ShapeDtypeStruct((T, D_MODEL), F32),
        compiler_params=_cparams(("parallel",)),
        name="gated_merge",
    )(h, g, ya, yb, oc, osel, ow, Fa, yd, wg, wb, wo, p3)


def _gate_expand_matrix():
    W = 4 * LANE
    m = np.zeros((LANE, 3 * W), np.float32)
    for j in range(4):
        for a in range(2):
            head = PAIR_HEAD[j, a]
            for r in range(3):
                c0 = r * W + j * LANE + a * HALF
                m[head * 3 + r, c0:c0 + HALF] = 1.0
    return jnp.asarray(m, BF16)


def _pair_rows(w):
    perm = np.concatenate([np.arange(HALF) + PAIR_HEAD[j, a] * HALF for j in range(4) for a in range(2)])
    return w[perm]


def kernel(x, norm_g, w_in, mla_q_norm, mla_kv_norm, mla_w_uq, mla_w_ukv, swa_sinks,
           nsa_cmp_pos, nsa_cmp_w1, nsa_cmp_w2, diff_lambda, diff_subln, rel_bias_table,
           w_branch, w_gate, w_o, ffn_w_gate, ffn_w_up, ffn_w_down, final_g):
    B, S, D = x.shape
    T = B * S
    depth = w_in.shape[0]
    t_att = 512
    t_swa = 128
    ncp = S // NSA_CMP_STRIDE
    nsel = S // NSA_SEL_LEN
    nselp = -(-nsel // LANE) * LANE

    tab_swa = rel_bias_table[:, :SWA_HEADS]
    tab_nsa = rel_bias_table[:, SWA_HEADS:SWA_HEADS + NSA_HEADS]
    tab_diff = rel_bias_table[:, SWA_HEADS + NSA_HEADS:]
    pair_heads = PAIR_HEAD.T.reshape(-1)
    bias_swa = _bias_tiles(tab_swa[:, pair_heads], t_swa).reshape(2, 4, 3, t_swa, t_swa)
    bias_nsa = _bias_tiles(tab_nsa[:, pair_heads], t_att).reshape(2, 4, 3, t_att, t_att)
    bias_diff = _bias_tiles(tab_diff, t_att)[None]
    ctab, stab = _rope_tables(S)

    ci = np.arange(ncp)[:, None]
    ni = np.arange(nselp)[None, :]
    nc_real = (S - NSA_CMP_LEN) // NSA_CMP_STRIDE + 1
    ovl = ((ci * NSA_CMP_STRIDE < ni * NSA_SEL_LEN + NSA_SEL_LEN)
           & (ni * NSA_SEL_LEN < ci * NSA_CMP_STRIDE + NSA_CMP_LEN) & (ci < nc_real) & (ni < nsel))
    ovl = jnp.asarray(ovl.astype(np.float32), BF16)
    emat = jnp.asarray((np.arange(S)[None, :] // NSA_SEL_LEN == np.arange(nselp)[:, None]).astype(np.float32), BF16)
    p3 = _gate_expand_matrix()

    h = x.reshape(T, D)
    for l in range(depth):
        gl = norm_g[l].reshape(3, 1, D)
        h = _ffn(h, gl[0], ffn_w_gate[l, 0].astype(BF16), ffn_w_up[l, 0].astype(BF16),
                 ffn_w_down[l, 0].astype(BF16), gl[0], final=False)

        Fa, Ga = _proj(h, gl[1], _proj_weight(w_in[l]))
        Fa = Fa.reshape(B, S, F_COLS)
        Ga = Ga.reshape(B, S, G_COLS)

        wa, wb_, wk, wv = _mla_weights(mla_w_uq[l], mla_w_ukv[l])
        Q, K, V = _mla_prep(Fa, ctab, stab, mla_q_norm[l][None], mla_kv_norm[l][None], wa, wb_, wk, wv)
        ya = _flash(Q, K, V, npairs=4, t=t_att, shared_qk=False,
                    q_blk=((0, 2), (1, 2)), k_blk=((0, 2), (1, 2)), v_blk=(0, 1))

        sink = swa_sinks[l][PAIR_HEAD]
        sink = jnp.broadcast_to(jnp.pad(sink, ((0, 0), (0, 6)))[:, :, None], (4, 8, LANE)).astype(F32)
        yb = _flash(Ga, Ga, Ga, npairs=4, t=t_swa, window=SWA_WINDOW,
                    q_blk=(GB_SQ, 1), k_blk=(GB_SK, 0), v_blk=(GB_SV, 0), bias=bias_swa, sinks=sink)

        r2 = jnp.stack([Fa[:, :, FB_NKC * LANE:(FB_NKC + 1) * LANE],
                        Fa[:, :, FB_NVC * LANE:(FB_NVC + 1) * LANE]]).reshape(2, B, ncp, NSA_CMP_STRIDE * LANE)
        ptop, pbot, w1e, w2e = _compress_weights(nsa_cmp_pos[l], nsa_cmp_w1[l], nsa_cmp_w2[l])
        kv2 = _compress(r2, ptop, pbot, w1e, w2e)
        oc, sel = _cmp_attn(Ga, kv2, ovl)
        osel = _flash(Ga, Ga, Ga, npairs=4, t=t_att, q_blk=(GB_NQ, 1), k_blk=(GB_NKS, 0), v_blk=(GB_NVS, 0),
                      bias=bias_nsa, sel=sel, emat=emat, out_dtype=F32)
        ow = _flash(Ga, Ga, Ga, npairs=4, t=t_att, window=NSA_WINDOW, q_blk=(GB_NQ, 1), k_blk=(GB_NKW, 0),
                    v_blk=(GB_NVW, 0), bias=bias_nsa, out_dtype=F32)

        lam_init = 0.8 - 0.6 * math.exp(-0.3 * l)
        lp = diff_lambda[l].astype(F32)
        lam = jnp.exp(jnp.sum(lp[0] * lp[1])) - jnp.exp(jnp.sum(lp[2] * lp[3])) + lam_init
        lam_arr = jnp.concatenate([jnp.full((1, LANE), 1.0, F32) * lam,
                                   jnp.full((7, LANE), 1.0 - lam_init, F32)], 0)
        yd = _flash(Ga, Ga, Ga, npairs=DIFF_HEADS, t=t_att, q_blk=(GB_DQ, 1), k_blk=(GB_DK, 1), v_blk=(GB_DV, 1),
                    bias=bias_diff, bias_shared=True, diff=(lam_arr, diff_subln[l][None]))

        wbr = jnp.stack([w_branch[l, 0], _pair_rows(w_branch[l, 1]), _pair_rows(w_branch[l, 2]),
                         w_branch[l, 3]]).astype(BF16)
        f2 = lambda a: a.reshape(T, a.shape[-1])
        h = _merge(h, gl[1], f2(ya), f2(yb), f2(oc), f2(osel), f2(ow), f2(Fa), f2(yd),
                   w_gate[l].astype(BF16), wbr, w_o[l].astype(BF16), p3)

        h = _ffn(h, gl[2], ffn_w_gate[l, 1].astype(BF16), ffn_w_up[l, 1].astype(BF16),
                 ffn_w_down[l, 1].astype(BF16), final_g[None], final=(l == depth - 1))
    return h.reshape(B, S, D)
```

```python
import functools
import math

import numpy as np
import jax
import jax.numpy as jnp
from jax import lax
from jax.experimental import pallas as pl
from jax.experimental.pallas import tpu as pltpu

F32 = jnp.float32
BF16 = jnp.bfloat16

D_MODEL = 1024
EPS = 1e-6
NEG = -1e30
D_FF = 2816

MLA_HEADS = 8
MLA_Q_RANK = 256
MLA_KV_RANK = 128
MLA_NOPE = 64
MLA_ROPE = 32
MLA_V = 64
ROPE_THETA = 10000.0

SWA_HEADS = 8
SWA_WINDOW = 128

NSA_HEADS = 8
NSA_HD = 64
NSA_CMP_LEN = 32
NSA_CMP_STRIDE = 16
NSA_CMP_HIDDEN = 128
NSA_SEL_LEN = 64
NSA_TOPK = 16
NSA_WINDOW = 512
NSA_FORCE = 1e4

DIFF_HEADS = 4
DIFF_HD = 64

N_BUCKETS = 32
MAX_DIST = 128

LANE = 128
HALF = 64
VMEM_LIMIT = 56 * 1024 * 1024
KEY_CHUNK = 256

O_CQ, O_CKV, O_KROPE, O_SQ, O_SK, O_SV, O_NQ = 0, 256, 384, 416, 928, 1056, 1184
O_NKC, O_NVC, O_NKS, O_NVS, O_NKW, O_NVW, O_NGATE = 1696, 1824, 1952, 2080, 2208, 2336, 2464
O_DQ, O_DK, O_DV = 2488, 3000, 3512

F_COLS = 1024
FB_CKV, FB_KROPE, FB_KSWAP, FB_NKC, FB_NVC, FB_NGATE = 2, 3, 4, 5, 6, 7
G_COLS = 3328
GB_SQ, GB_NQ, GB_DQ, GB_DK, GB_DV = 0, 4, 8, 12, 16
GB_SK, GB_SV, GB_NKS, GB_NVS, GB_NKW, GB_NVW = 20, 21, 22, 23, 24, 25


def _cparams(sem):
    return pltpu.CompilerParams(dimension_semantics=sem, vmem_limit_bytes=VMEM_LIMIT)


def _rms(x, g):
    return x * lax.rsqrt(jnp.mean(x * x, -1, keepdims=True) + EPS) * g


def _dot(a, b):
    return jnp.dot(a, b, preferred_element_type=F32)


def _dot_nt(a, b):
    return lax.dot_general(a, b, (((1,), (1,)), ((), ())), preferred_element_type=F32)


def _split3(x):
    hi = x.astype(BF16)
    r = x - hi.astype(F32)
    mid = r.astype(BF16)
    lo = (r - mid.astype(F32)).astype(BF16)
    return hi, mid, lo


def _dot_exact(x, w01):
    hi, mid, lo = _split3(x)
    return _dot(hi, w01) + _dot(mid, w01) + _dot(lo, w01)


def _ffn_kernel(h_ref, g_ref, wg_ref, wu_ref, wd_ref, gf_ref, o_ref, n_scr, acc_scr, *, final):
    j = pl.program_id(1)

    @pl.when(j == 0)
    def _():
        n_scr[...] = _rms(h_ref[...], g_ref[...]).astype(BF16)
        acc_scr[...] = jnp.zeros_like(acc_scr)

    n = n_scr[...]
    a = _dot(n, wg_ref[...])
    b = _dot(n, wu_ref[...])
    t = (a * jax.nn.sigmoid(a) * b).astype(BF16)
    acc_scr[...] += _dot(t, wd_ref[...])

    @pl.when(j == pl.num_programs(1) - 1)
    def _():
        out = h_ref[...] + 0.5 * acc_scr[...]
        if final:
            out = _rms(out, gf_ref[...])
        o_ref[...] = out


def _ffn(h, g, wg, wu, wd, gf, final):
    T = h.shape[0]
    tm, tf = 512, D_FF // 2
    return pl.pallas_call(
        functools.partial(_ffn_kernel, final=final),
        grid=(T // tm, D_FF // tf),
        in_specs=[
            pl.BlockSpec((tm, D_MODEL), lambda i, j: (i, 0)),
            pl.BlockSpec((1, D_MODEL), lambda i, j: (0, 0)),
            pl.BlockSpec((D_MODEL, tf), lambda i, j: (0, j)),
            pl.BlockSpec((D_MODEL, tf), lambda i, j: (0, j)),
            pl.BlockSpec((tf, D_MODEL), lambda i, j: (j, 0)),
            pl.BlockSpec((1, D_MODEL), lambda i, j: (0, 0)),
        ],
        out_specs=pl.BlockSpec((tm, D_MODEL), lambda i, j: (i, 0)),
        out_shape=jax.ShapeDtypeStruct((T, D_MODEL), F32),
        scratch_shapes=[pltpu.VMEM((tm, D_MODEL), BF16), pltpu.VMEM((tm, D_MODEL), F32)],
        compiler_params=_cparams(("parallel", "arbitrary")),
        name="ffn_half_step",
    )(h, g, wg, wu, wd, gf)


def _proj_kernel(h_ref, g_ref, w_ref, f_ref, gq_ref):
    u = _rms(h_ref[...], g_ref[...]).astype(BF16)
    f_ref[...] = _dot(u, w_ref[:, :F_COLS])
    step = 512
    for c0 in range(0, G_COLS, step):
        c1 = min(c0 + step, G_COLS)
        gq_ref[:, c0:c1] = _dot(u, w_ref[:, F_COLS + c0:F_COLS + c1]).astype(BF16)


def _proj(h, g, wp):
    T = h.shape[0]
    tm = 512
    return pl.pallas_call(
        _proj_kernel,
        grid=(T // tm,),
        in_specs=[
            pl.BlockSpec((tm, D_MODEL), lambda i: (i, 0)),
            pl.BlockSpec((1, D_MODEL), lambda i: (0, 0)),
            pl.BlockSpec((D_MODEL, F_COLS + G_COLS), lambda i: (0, 0)),
        ],
        out_specs=[
            pl.BlockSpec((tm, F_COLS), lambda i: (i, 0)),
            pl.BlockSpec((tm, G_COLS), lambda i: (i, 0)),
        ],
        out_shape=[jax.ShapeDtypeStruct((T, F_COLS), F32),
                   jax.ShapeDtypeStruct((T, G_COLS), BF16)],
        compiler_params=_cparams(("parallel",)),
        name="in_proj",
    )(h, g, wp)


def _proj_weight(w_in):
    ncol = F_COLS + G_COLS
    src = np.zeros((ncol,), np.int32)
    scale = np.zeros((ncol,), np.float32)

    def put(dst, srcs, s=1.0):
        srcs = np.asarray(srcs)
        src[dst:dst + len(srcs)] = srcs
        scale[dst:dst + len(srcs)] = s

    half = MLA_ROPE // 2
    put(0, O_CQ + np.arange(MLA_Q_RANK))
    put(FB_CKV * LANE, O_CKV + np.arange(MLA_KV_RANK))
    put(FB_KROPE * LANE + MLA_NOPE, O_KROPE + np.arange(MLA_ROPE))
    put(FB_KSWAP * LANE + MLA_NOPE, O_KROPE + np.concatenate([np.arange(half, MLA_ROPE), np.arange(half)]))
    put(FB_NKC * LANE, O_NKC + np.arange(LANE))
    put(FB_NVC * LANE, O_NVC + np.arange(LANE))
    put(FB_NGATE * LANE, O_NGATE + np.arange(3 * NSA_HEADS))
    g0 = F_COLS
    pair_perm = np.concatenate([np.concatenate([(j) * HALF + np.arange(HALF), (4 + j) * HALF + np.arange(HALF)])
                                for j in range(4)])
    put(g0 + GB_SQ * LANE, O_SQ + pair_perm, 0.125)
    put(g0 + GB_NQ * LANE, O_NQ + pair_perm, 0.125)
    put(g0 + GB_DQ * LANE, O_DQ + np.arange(512), 0.125)
    put(g0 + GB_DK * LANE, O_DK + np.arange(512))
    put(g0 + GB_DV * LANE, O_DV + np.arange(512))
    for blk, off in ((GB_SK, O_SK), (GB_SV, O_SV), (GB_NKS, O_NKS), (GB_NVS, O_NVS),
                     (GB_NKW, O_NKW), (GB_NVW, O_NVW)):
        put(g0 + blk * LANE, off + np.arange(LANE))
    return (w_in[:, src] * scale[None, :]).astype(BF16)


PAIR_HEAD = np.array([[j, 4 + j] for j in range(4)])


def _mla_prep_kernel(cq_ref, ckv_ref, kr_ref, ks_ref, ct_ref, st_ref, qg_ref, kg_ref,
                     wa_ref, wb_ref, wk_ref, wv_ref, q_ref, k_ref, v_ref, *, scale):
    qn = _rms(cq_ref[0], qg_ref[...]).astype(BF16)
    kn = _rms(ckv_ref[0], kg_ref[...]).astype(BF16)
    ct = ct_ref[...]
    st = st_ref[...]
    krot = kr_ref[0] * ct + ks_ref[0] * st
    v_ref[0] = _dot(kn, wv_ref[...]).astype(BF16)
    for h in range(MLA_HEADS):
        sl = slice(h * LANE, (h + 1) * LANE)
        qa = _dot(qn, wa_ref[:, sl])
        qb = _dot(qn, wb_ref[:, sl])
        q_ref[0, :, sl] = ((qa * ct + qb * st) * scale).astype(BF16)
        k_ref[0, :, sl] = (_dot(kn, wk_ref[:, sl]) + krot).astype(BF16)


def _mla_prep(Fa, ctab, stab, qg, kg, wa, wb, wk, wv):
    B, S, _ = Fa.shape
    tm = 512
    HW = MLA_HEADS * LANE
    full = lambda shape: pl.BlockSpec(shape, lambda b, i: (0,) * len(shape))
    return pl.pallas_call(
        functools.partial(_mla_prep_kernel, scale=(MLA_NOPE + MLA_ROPE) ** -0.5),
        grid=(B, S // tm),
        in_specs=[
            pl.BlockSpec((1, tm, MLA_Q_RANK), lambda b, i: (b, i, 0)),
            pl.BlockSpec((1, tm, LANE), lambda b, i: (b, i, FB_CKV)),
            pl.BlockSpec((1, tm, LANE), lambda b, i: (b, i, FB_KROPE)),
            pl.BlockSpec((1, tm, LANE), lambda b, i: (b, i, FB_KSWAP)),
            pl.BlockSpec((tm, LANE), lambda b, i: (i, 0)),
            pl.BlockSpec((tm, LANE), lambda b, i: (i, 0)),
            full((1, MLA_Q_RANK)), full((1, MLA_KV_RANK)),
            full((MLA_Q_RANK, HW)), full((MLA_Q_RANK, HW)),
            full((MLA_KV_RANK, HW)), full((MLA_KV_RANK, MLA_HEADS * MLA_V)),
        ],
        out_specs=[
            pl.BlockSpec((1, tm, HW), lambda b, i: (b, i, 0)),
            pl.BlockSpec((1, tm, HW), lambda b, i: (b, i, 0)),
            pl.BlockSpec((1, tm, MLA_HEADS * MLA_V), lambda b, i: (b, i, 0)),
        ],
        out_shape=[jax.ShapeDtypeStruct((B, S, HW), BF16),
                   jax.ShapeDtypeStruct((B, S, HW), BF16),
                   jax.ShapeDtypeStruct((B, S, MLA_HEADS * MLA_V), BF16)],
        compiler_params=_cparams(("parallel", "parallel")),
        name="mla_prep",
    )(Fa, Fa, Fa, Fa, ctab, stab, qg, kg, wa, wb, wk, wv)


def _mla_weights(w_uq, w_ukv):
    dq = MLA_NOPE + MLA_ROPE
    half = MLA_ROPE // 2
    z = lambda n: jnp.zeros((w_uq.shape[0], n), F32)
    wa, wb, wk, wv = [], [], [], []
    for h in range(MLA_HEADS):
        nope = w_uq[:, h * dq:h * dq + MLA_NOPE]
        r = w_uq[:, h * dq + MLA_NOPE:(h + 1) * dq]
        wa += [nope, r, z(LANE - dq)]
        wb += [z(MLA_NOPE), r[:, half:], r[:, :half], z(LANE - dq)]
        wk += [w_ukv[:, h * LANE:h * LANE + MLA_NOPE], jnp.zeros((w_ukv.shape[0], LANE - MLA_NOPE), F32)]
        wv += [w_ukv[:, h * LANE + MLA_NOPE:(h + 1) * LANE]]
    cat = lambda xs: jnp.concatenate(xs, axis=1).astype(BF16)
    return cat(wa), cat(wb), cat(wk), cat(wv)


def _rope_tables(S):
    half = MLA_ROPE // 2
    freqs = ROPE_THETA ** (-jnp.arange(half, dtype=F32) / half)
    ang = jnp.arange(S, dtype=jnp.int32)[:, None].astype(F32) * freqs
    cos, sin = jnp.cos(ang), jnp.sin(ang)
    ones = jnp.ones((S, MLA_NOPE), F32)
    zeros = jnp.zeros((S, LANE - MLA_NOPE - MLA_ROPE), F32)
    ctab = jnp.concatenate([ones, cos, cos, zeros], 1)
    stab = jnp.concatenate([jnp.zeros((S, MLA_NOPE), F32), -sin, sin, zeros], 1)
    return ctab, stab


def _t5_bucket(dist):
    n = jnp.maximum(dist, 0)
    max_exact = N_BUCKETS // 2
    nf = jnp.maximum(n, 1).astype(F32)
    large = max_exact + (jnp.log(nf / max_exact) / math.log(MAX_DIST / max_exact)
                         * (N_BUCKETS - max_exact)).astype(jnp.int32)
    large = jnp.minimum(large, N_BUCKETS - 1)
    return jnp.where(n < max_exact, n, large)


def _bias_tiles(tab, t):
    assert t >= MAX_DIST
    i = jnp.arange(t, dtype=jnp.int32)[:, None]
    j = jnp.arange(t, dtype=jnp.int32)[None, :]
    tiles = []
    tab_t = tab.T[:, :, None, None]
    for rel in range(2):
        b = _t5_bucket(rel * t + j - i)[None]
        tile = jnp.zeros((tab.shape[1], t, t), F32)
        for n in range(N_BUCKETS):
            tile = jnp.where(b == n, tab_t[:, n], tile)
        tiles.append(tile)
    far = jnp.broadcast_to(tab[N_BUCKETS - 1][:, None, None], (tab.shape[1], t, t))
    tiles.append(far)
    return jnp.stack(tiles, 1).astype(F32)


def _flash_kernel(*refs, t, kc, window, shared_qk, has_bias, bias_shared, has_sel, has_sink, diff, out_dtype):
    qt_ref, kt_ref, first_ref, last_ref, rel_ref = refs[:5]
    refs = list(refs[5:])
    if shared_qk:
        q_ref, k_ref = refs[:2]
        refs = refs[2:]
    else:
        qa_ref, qb_ref, ka_ref, kb_ref = refs[:4]
        refs = refs[4:]
    vt_ref = refs.pop(0)
    if has_bias:
        ba_ref = refs.pop(0)
        bb_ref = ba_ref if bias_shared else refs.pop(0)
    if has_sel:
        sa_ref, sb_ref, et_ref = refs[:3]
        refs = refs[3:]
    if has_sink:
        sink_ref = refs.pop(0)
    if diff:
        lam_ref, sub_ref = refs[:2]
        refs = refs[2:]
    o_ref, m_scr, l_scr, acc_scr, p_scr = refs

    step = pl.program_id(2)
    qt = qt_ref[step]
    kt = kt_ref[step]
    lo = lax.broadcasted_iota(jnp.int32, (1, LANE), 1) < HALF

    @pl.when(first_ref[step] == 1)
    def _():
        for a in range(2):
            if has_sink:
                m_scr[a] = jnp.broadcast_to(sink_ref[0, a:a + 1, 0:1], (1, t))
                l_scr[a] = jnp.ones((1, t), F32)
            else:
                m_scr[a] = jnp.full((1, t), NEG, F32)
                l_scr[a] = jnp.zeros((1, t), F32)
            acc_scr[a] = jnp.zeros((LANE, t), F32)

    if shared_qk:
        q = q_ref[0]
        zero = jnp.zeros_like(q)
        qs = (jnp.where(lo, q, zero), jnp.where(lo, zero, q))
        ks = (k_ref[0], k_ref[0])
    else:
        qs = (qa_ref[0], qb_ref[0])
        ks = (ka_ref[0], kb_ref[0])
    vt = vt_ref[0]

    def process(diag):
        jobs = [(kh, a) for kh in range(0, t, kc) for a in range(2)]

        def scores(job):
            kh, a = job
            st = _dot_nt(ks[a][kh:kh + kc], qs[a])
            pick = _dot_nt(et_ref[kh:kh + kc, :], (sa_ref if a == 0 else sb_ref)[0]) if has_sel else None
            return st, pick

        col_slices = [slice(c, c + LANE) for c in range(0, t, LANE)]
        m = [[m_scr[a, :, cs] for cs in col_slices] for a in range(2)]
        l = [[l_scr[a, :, cs] for cs in col_slices] for a in range(2)]
        acc = [acc_scr[0], acc_scr[1]]
        nxt = scores(jobs[0])
        for ji, (kh, a) in enumerate(jobs):
            keys = slice(kh, kh + kc)
            if True:
                st, pick = nxt
                if ji + 1 < len(jobs):
                    nxt = scores(jobs[ji + 1])
                b_ref = (ba_ref if a == 0 else bb_ref) if has_bias else None
                alphas = []
                for ci, cols in enumerate(col_slices):
                    c = cols.start
                    base = (0 if diag else t) + c - kh
                    dmin, dmax = base - (kc - 1), base + (LANE - 1)
                    if diag:
                        none_valid, all_valid = dmax < 0, dmin >= 0
                    elif window is not None:
                        none_valid, all_valid = dmin >= window, dmax < window
                    else:
                        none_valid, all_valid = False, True
                    if none_valid:
                        alphas.append(jnp.ones((1, LANE), F32))
                        p_scr[a, keys, cols] = jnp.zeros((kc, LANE), BF16)
                        continue
                    s = st[:, cols]
                    if has_bias:
                        s = s + b_ref[0, 0, 0, keys, cols]
                    valid = None
                    if not all_valid:
                        d = (lax.broadcasted_iota(jnp.int32, (kc, LANE), 1)
                             - lax.broadcasted_iota(jnp.int32, (kc, LANE), 0)) + base
                        valid = (d >= 0) if diag else (d < window)
                    if has_sel:
                        hit = pick[:, cols] > 0.5
                        valid = hit if valid is None else (valid & hit)
                    if valid is not None:
                        s = jnp.where(valid, s, NEG)
                    m_prev = m[a][ci]
                    m_new = jnp.maximum(m_prev, jnp.max(s, 0, keepdims=True))
                    alpha = jnp.exp(m_prev - m_new)
                    p = jnp.exp(s - m_new)
                    l[a][ci] = alpha * l[a][ci] + jnp.sum(p, 0, keepdims=True)
                    m[a][ci] = m_new
                    alphas.append(alpha)
                    p_scr[a, keys, cols] = p.astype(BF16)
                acc[a] = jnp.concatenate(alphas, axis=1) * acc[a] + _dot(vt[:, keys], p_scr[a, keys, :])
        for a in range(2):
            for ci, cs in enumerate(col_slices):
                m_scr[a, :, cs] = m[a][ci]
                l_scr[a, :, cs] = l[a][ci]
            acc_scr[a] = acc[a]

    @pl.when(qt == kt)
    def _():
        process(True)

    @pl.when(qt != kt)
    def _():
        process(False)

    @pl.when(last_ref[step] == 1)
    def _():
        o0 = acc_scr[0] / l_scr[0]
        o1 = acc_scr[1] / l_scr[1]
        if diff:
            o = (o0 - lam_ref[0:1, 0:1] * o1).T
            o = _rms(o, sub_ref[...]) * lam_ref[1:2, 0:1]
        else:
            feat = lax.broadcasted_iota(jnp.int32, (LANE, 1), 0)
            o = jnp.where(feat < HALF, o0, o1).T
        o_ref[0] = o.astype(out_dtype)


def _flash(q_arr, k_arr, vt_arr, *, npairs, t, q_blk, k_blk, v_blk, window=None, shared_qk=True,
           bias=None, bias_shared=False, sel=None, emat=None, sinks=None, diff=None, out_dtype=BF16):
    B, S, _ = q_arr.shape
    nq = S // t
    nprev = None if window is None else -(-(window - 1) // t)
    assert window is None or t <= window <= t + 1, "a window must cover the diagonal tile and end in the previous one"
    steps = []
    for qi in range(nq):
        k0 = 0 if nprev is None else max(0, qi - nprev)
        for ki in range(k0, qi + 1):
            steps.append((qi, ki, int(ki == k0), int(ki == qi), min(qi - ki, 2)))
    tabs = [jnp.asarray(np.array([s[c] for s in steps], np.int32)) for c in range(5)]
    nsteps = len(steps)
    kc = min(t, KEY_CHUNK)

    def qmap(off, stride):
        return lambda b, p, s, qt, kt, fr, la, rl: (b, qt[s], off + stride * p)

    def kmap(off, stride):
        return lambda b, p, s, qt, kt, fr, la, rl: (b, kt[s], off + stride * p)

    in_specs, args = [], []
    if shared_qk:
        in_specs += [pl.BlockSpec((1, t, LANE), qmap(*q_blk)), pl.BlockSpec((1, t, LANE), kmap(*k_blk))]
        args += [q_arr, k_arr]
    else:
        in_specs += [pl.BlockSpec((1, t, LANE), qmap(*q_blk[0])), pl.BlockSpec((1, t, LANE), qmap(*q_blk[1])),
                     pl.BlockSpec((1, t, LANE), kmap(*k_blk[0])), pl.BlockSpec((1, t, LANE), kmap(*k_blk[1]))]
        args += [q_arr, q_arr, k_arr, k_arr]
    voff, vstride = v_blk
    in_specs.append(pl.BlockSpec((1, LANE, t),
                                 lambda b, p, s, qt, kt, fr, la, rl: (b, voff + vstride * p, kt[s])))
    args.append(vt_arr)
    if bias is not None:
        for a in range(1 if bias_shared else 2):
            in_specs.append(pl.BlockSpec((1, 1, 1, t, t),
                                         lambda b, p, s, qt, kt, fr, la, rl, a=a: (a, p, rl[s], 0, 0)))
            args.append(bias)
    if sel is not None:
        nselp = emat.shape[1]
        for a in range(2):
            in_specs.append(pl.BlockSpec((1, t, nselp),
                                         lambda b, p, s, qt, kt, fr, la, rl, a=a: (b, qt[s], a)))
            args.append(sel)
        in_specs.append(pl.BlockSpec((t, nselp), lambda b, p, s, qt, kt, fr, la, rl: (kt[s], 0)))
        args.append(emat)
    if sinks is not None:
        in_specs.append(pl.BlockSpec((1, 8, LANE), lambda b, p, s, qt, kt, fr, la, rl: (p, 0, 0)))
        args.append(sinks)
    if diff is not None:
        lam_arr, sub = diff
        in_specs.append(pl.BlockSpec((8, LANE), lambda b, p, s, qt, kt, fr, la, rl: (0, 0)))
        in_specs.append(pl.BlockSpec((1, LANE), lambda b, p, s, qt, kt, fr, la, rl: (0, 0)))
        args += [lam_arr, sub]

    kern = functools.partial(
        _flash_kernel, t=t, kc=kc, window=window, shared_qk=shared_qk, has_bias=bias is not None,
        bias_shared=bias_shared, has_sel=sel is not None, has_sink=sinks is not None,
        diff=diff is not None, out_dtype=out_dtype)
    return pl.pallas_call(
        kern,
        grid_spec=pltpu.PrefetchScalarGridSpec(
            num_scalar_prefetch=5,
            grid=(B, npairs, nsteps),
            in_specs=in_specs,
            out_specs=pl.BlockSpec((1, t, LANE), lambda b, p, s, qt, kt, fr, la, rl: (b, qt[s], p)),
            scratch_shapes=[pltpu.VMEM((2, 1, t), F32), pltpu.VMEM((2, 1, t), F32),
                            pltpu.VMEM((2, LANE, t), F32), pltpu.VMEM((2, t, t), BF16)],
        ),
        out_shape=jax.ShapeDtypeStruct((B, S, npairs * LANE), out_dtype),
        compiler_params=_cparams(("parallel", "parallel", "arbitrary")),
        name="flash_pairs",
    )(*tabs, *args)


def _compress_kernel(r_ref, pt_ref, pb_ref, w1_ref, w2_ref, o_ref):
    r = r_ref[0, 0]
    n = r.shape[0]
    xt = (r + pt_ref[0]).astype(BF16)
    xb = (r + pb_ref[0]).astype(BF16)
    out = jnp.zeros((n, LANE), F32)
    for h in range(2):
        top = _dot(xt, w1_ref[0, h, 0])
        bot = _dot(xb, w1_ref[0, h, 1])
        x = top + pltpu.roll(bot, n - 1, 0)
        hd = 0.5 * x * (1.0 + jnp.tanh(math.sqrt(2.0 / math.pi) * (x + 0.044715 * (x * x * x))))
        out = out + _dot(hd.astype(BF16), w2_ref[0, h])
    o_ref[0, 0] = out.astype(BF16)


def _compress(r2, ptop, pbot, w1e, w2e):
    _, B, n, K = r2.shape
    return pl.pallas_call(
        _compress_kernel,
        grid=(2, B),
        in_specs=[
            pl.BlockSpec((1, 1, n, K), lambda c, b: (c, b, 0, 0)),
            pl.BlockSpec((1, 1, K), lambda c, b: (c, 0, 0)),
            pl.BlockSpec((1, 1, K), lambda c, b: (c, 0, 0)),
            pl.BlockSpec((1, 2, 2, K, NSA_CMP_HIDDEN), lambda c, b: (c, 0, 0, 0, 0)),
            pl.BlockSpec((1, 2, NSA_CMP_HIDDEN, LANE), lambda c, b: (c, 0, 0, 0)),
        ],
        out_specs=pl.BlockSpec((1, 1, n, LANE), lambda c, b: (c, b, 0, 0)),
        out_shape=jax.ShapeDtypeStruct((2, B, n, LANE), BF16),
        compiler_params=_cparams(("parallel", "parallel")),
        name="nsa_compress",
    )(r2, ptop, pbot, w1e, w2e)


def _compress_weights(cmp_pos, cmp_w1, cmp_w2):
    L2 = NSA_CMP_STRIDE
    d = NSA_HD
    w1 = cmp_w1.reshape(2, 2, L2, d, NSA_CMP_HIDDEN)
    z = jnp.zeros_like(w1)
    e0 = jnp.concatenate([w1, z], axis=3)
    e1 = jnp.concatenate([z, w1], axis=3)
    w1e = jnp.stack([e0, e1], axis=1).reshape(2, 2, 2, L2 * 2 * d, NSA_CMP_HIDDEN).astype(BF16)
    pos = cmp_pos.reshape(2, 2, L2, 1, d)
    pos = jnp.broadcast_to(pos, (2, 2, L2, 2, d)).reshape(2, 2, 1, L2 * 2 * d)
    zw = jnp.zeros_like(cmp_w2)
    w2e = jnp.stack([jnp.concatenate([cmp_w2, zw], -1), jnp.concatenate([zw, cmp_w2], -1)], 1).astype(BF16)
    return pos[:, 0], pos[:, 1], w1e, w2e


def _cmp_attn_kernel(q_ref, kc_ref, vc_ref, ov_ref, oc_ref, sel_ref, *, tq, ncp, nselp):
    i = pl.program_id(1)
    qpos = i * tq + lax.broadcasted_iota(jnp.int32, (tq, 1), 0)
    cidx = lax.broadcasted_iota(jnp.int32, (1, ncp), 1)
    cvalid = (cidx * NSA_CMP_STRIDE + (NSA_CMP_LEN - 1)) <= qpos
    lane = lax.broadcasted_iota(jnp.int32, (1, LANE), 1)
    lo = lane < HALF
    kc = kc_ref[0, 0]
    vc = vc_ref[0, 0]
    psum = [jnp.zeros((tq, ncp), F32), jnp.zeros((tq, ncp), F32)]
    for j in range(4):
        q2 = q_ref[0, :, j * LANE:(j + 1) * LANE]
        zero = jnp.zeros_like(q2)
        outs = []
        for a in range(2):
            qa = jnp.where(lo, q2, zero) if a == 0 else jnp.where(lo, zero, q2)
            s = jnp.where(cvalid, _dot_nt(qa, kc), NEG)
            m = jnp.max(s, -1, keepdims=True)
            e = jnp.where(cvalid, jnp.exp(s - m), 0.0)
            den = jnp.sum(e, -1, keepdims=True)
            p = e / jnp.where(den > 0, den, 1.0)
            psum[a] = psum[a] + p
            outs.append(_dot(p.astype(BF16), vc))
        oc_ref[0, :, j * LANE:(j + 1) * LANE] = jnp.where(lo, outs[0], outs[1])

    n = lax.broadcasted_iota(jnp.int32, (1, nselp), 1)
    cur = qpos >> int(math.log2(NSA_SEL_LEN))
    causal = (n * NSA_SEL_LEN) <= qpos
    forced = (n == 0) | (n == cur) | (n == cur - 1)
    nf = n.astype(F32)
    for a in range(2):
        imp = _dot_exact(psum[a], ov_ref[...])
        x = jnp.where(causal, jnp.where(forced, NSA_FORCE, imp), -1.0)
        picked = jnp.zeros((tq, nselp), F32)
        for _ in range(NSA_TOPK):
            mx = jnp.max(x, -1, keepdims=True)
            idx = jnp.min(jnp.where(x == mx, nf, float(nselp)), -1, keepdims=True)
            hit = nf == idx
            picked = jnp.where(hit, 1.0, picked)
            x = jnp.where(hit, -2.0, x)
        sel_ref[0, :, a * nselp:(a + 1) * nselp] = jnp.where(causal, picked, 0.0).astype(BF16)


def _cmp_attn(G3, kv2, ovl):
    B, S, _ = G3.shape
    ncp = kv2.shape[2]
    nselp = ovl.shape[1]
    tq = 256
    return pl.pallas_call(
        functools.partial(_cmp_attn_kernel, tq=tq, ncp=ncp, nselp=nselp),
        grid=(B, S // tq),
        in_specs=[
            pl.BlockSpec((1, tq, 4 * LANE), lambda b, i: (b, i, GB_NQ // 4)),
            pl.BlockSpec((1, 1, ncp, LANE), lambda b, i: (0, b, 0, 0)),
            pl.BlockSpec((1, 1, ncp, LANE), lambda b, i: (1, b, 0, 0)),
            pl.BlockSpec((ncp, nselp), lambda b, i: (0, 0)),
        ],
        out_specs=[
            pl.BlockSpec((1, tq, 4 * LANE), lambda b, i: (b, i, 0)),
            pl.BlockSpec((1, tq, 2 * nselp), lambda b, i: (b, i, 0)),
        ],
        out_shape=[jax.ShapeDtypeStruct((B, S, 4 * LANE), F32),
                   jax.ShapeDtypeStruct((B, S, 2 * nselp), BF16)],
        compiler_params=_cparams(("parallel", "parallel")),
        name="nsa_cmp_attn_topk",
    )(G3, kv2, kv2, ovl)


def _merge_kernel(h_ref, g_ref, ya_ref, yb_ref, oc_ref, os_ref, ow_ref, ng_ref, yd_ref,
                  wg_ref, wb_ref, wo_ref, p3_ref, o_ref):
    h = h_ref[...]
    u = _rms(h, g_ref[...]).astype(BF16)
    gexp = _dot_exact(jax.nn.sigmoid(ng_ref[...]), p3_ref[...])
    W = 4 * LANE
    yc = gexp[:, :W] * oc_ref[...] + gexp[:, W:2 * W] * os_ref[...] + gexp[:, 2 * W:] * ow_ref[...]
    ys = (ya_ref[...], yb_ref[...], yc.astype(BF16), yd_ref[...])
    merged = jnp.zeros(h.shape, F32)
    for i in range(4):
        gate = jax.nn.sigmoid(_dot(u, wg_ref[i]))
        merged = merged + gate * _dot(ys[i], wb_ref[i])
    o_ref[...] = h + _dot(merged.astype(BF16), wo_ref[...])


def _merge(h, g, ya, yb, oc, osel, ow, Fa, yd, wg, wb, wo, p3):
    T = h.shape[0]
    tm = 256
    W = 4 * LANE
    row = lambda w: pl.BlockSpec((tm, w), lambda i: (i, 0))
    const = lambda shape: pl.BlockSpec(shape, lambda i: (0,) * len(shape))
    return pl.pallas_call(
        _merge_kernel,
        grid=(T // tm,),
        in_specs=[
            row(D_MODEL), const((1, D_MODEL)), row(W), row(W), row(W), row(W), row(W),
            pl.BlockSpec((tm, LANE), lambda i: (i, FB_NGATE)), row(W),
            const((4, D_MODEL, D_MODEL)), const((4, W, D_MODEL)), const((D_MODEL, D_MODEL)),
            const((LANE, 3 * W)),
        ],
        out_specs=row(D_MODEL),
        out_shape=jax.ShapeDtypeStruct((T, D_MODEL), F32),
        compiler_params=_cparams(("parallel",)),
        name="gated_merge",
    )(h, g, ya, yb, oc, osel, ow, Fa, yd, wg, wb, wo, p3)


def _gate_expand_matrix():
    W = 4 * LANE
    m = np.zeros((LANE, 3 * W), np.float32)
    for j in range(4):
        for a in range(2):
            head = PAIR_HEAD[j, a]
            for r in range(3):
                c0 = r * W + j * LANE + a * HALF
                m[head * 3 + r, c0:c0 + HALF] = 1.0
    return jnp.asarray(m, BF16)


def _pair_rows(w):
    perm = np.concatenate([np.arange(HALF) + PAIR_HEAD[j, a] * HALF for j in range(4) for a in range(2)])
    return w[perm]


def kernel(x, norm_g, w_in, mla_q_norm, mla_kv_norm, mla_w_uq, mla_w_ukv, swa_sinks,
           nsa_cmp_pos, nsa_cmp_w1, nsa_cmp_w2, diff_lambda, diff_subln, rel_bias_table,
           w_branch, w_gate, w_o, ffn_w_gate, ffn_w_up, ffn_w_down, final_g):
    B, S, D = x.shape
    T = B * S
    depth = w_in.shape[0]
    t_att = 512
    t_swa = 128
    ncp = S // NSA_CMP_STRIDE
    nsel = S // NSA_SEL_LEN
    nselp = -(-nsel // LANE) * LANE

    tab_swa = rel_bias_table[:, :SWA_HEADS]
    tab_nsa = rel_bias_table[:, SWA_HEADS:SWA_HEADS + NSA_HEADS]
    tab_diff = rel_bias_table[:, SWA_HEADS + NSA_HEADS:]
    pair_heads = PAIR_HEAD.T.reshape(-1)
    bias_swa = _bias_tiles(tab_swa[:, pair_heads], t_swa).reshape(2, 4, 3, t_swa, t_swa)
    bias_nsa = _bias_tiles(tab_nsa[:, pair_heads], t_att).reshape(2, 4, 3, t_att, t_att)
    bias_diff = _bias_tiles(tab_diff, t_att)[None]
    ctab, stab = _rope_tables(S)

    ci = np.arange(ncp)[:, None]
    ni = np.arange(nselp)[None, :]
    nc_real = (S - NSA_CMP_LEN) // NSA_CMP_STRIDE + 1
    ovl = ((ci * NSA_CMP_STRIDE < ni * NSA_SEL_LEN + NSA_SEL_LEN)
           & (ni * NSA_SEL_LEN < ci * NSA_CMP_STRIDE + NSA_CMP_LEN) & (ci < nc_real) & (ni < nsel))
    ovl = jnp.asarray(ovl.astype(np.float32), BF16)
    emat = jnp.asarray((np.arange(S)[:, None] // NSA_SEL_LEN == np.arange(nselp)[None, :]).astype(np.float32), BF16)
    p3 = _gate_expand_matrix()

    h = x.reshape(T, D)
    for l in range(depth):
        gl = norm_g[l].reshape(3, 1, D)
        h = _ffn(h, gl[0], ffn_w_gate[l, 0].astype(BF16), ffn_w_up[l, 0].astype(BF16),
                 ffn_w_down[l, 0].astype(BF16), gl[0], final=False)

        Fa, Ga = _proj(h, gl[1], _proj_weight(w_in[l]))
        Fa = Fa.reshape(B, S, F_COLS)
        Ga = Ga.reshape(B, S, G_COLS)

        wa, wb_, wk, wv = _mla_weights(mla_w_uq[l], mla_w_ukv[l])
        Q, K, V = _mla_prep(Fa, ctab, stab, mla_q_norm[l][None], mla_kv_norm[l][None], wa, wb_, wk, wv)
        ya = _flash(Q, K, V.transpose(0, 2, 1), npairs=4, t=t_att, shared_qk=False,
                    q_blk=((0, 2), (1, 2)), k_blk=((0, 2), (1, 2)), v_blk=(0, 1))

        sink = swa_sinks[l][PAIR_HEAD]
        sink = jnp.broadcast_to(jnp.pad(sink, ((0, 0), (0, 6)))[:, :, None], (4, 8, LANE)).astype(F32)
        GvT = Ga[:, :, GB_DV * LANE:].transpose(0, 2, 1)
        yb = _flash(Ga, Ga, GvT, npairs=4, t=t_swa, window=SWA_WINDOW,
                    q_blk=(GB_SQ, 1), k_blk=(GB_SK, 0), v_blk=(GB_SV - GB_DV, 0), bias=bias_swa, sinks=sink)

        r2 = jnp.stack([Fa[:, :, FB_NKC * LANE:(FB_NKC + 1) * LANE],
                        Fa[:, :, FB_NVC * LANE:(FB_NVC + 1) * LANE]]).reshape(2, B, ncp, NSA_CMP_STRIDE * LANE)
        ptop, pbot, w1e, w2e = _compress_weights(nsa_cmp_pos[l], nsa_cmp_w1[l], nsa_cmp_w2[l])
        kv2 = _compress(r2, ptop, pbot, w1e, w2e)
        oc, sel = _cmp_attn(Ga, kv2, ovl)
        osel = _flash(Ga, Ga, GvT, npairs=4, t=t_att, q_blk=(GB_NQ, 1), k_blk=(GB_NKS, 0),
                      v_blk=(GB_NVS - GB_DV, 0), bias=bias_nsa, sel=sel, emat=emat, out_dtype=F32)
        ow = _flash(Ga, Ga, GvT, npairs=4, t=t_att, window=NSA_WINDOW, q_blk=(GB_NQ, 1), k_blk=(GB_NKW, 0),
                    v_blk=(GB_NVW - GB_DV, 0), bias=bias_nsa, out_dtype=F32)

        lam_init = 0.8 - 0.6 * math.exp(-0.3 * l)
        lp = diff_lambda[l].astype(F32)
        lam = jnp.exp(jnp.sum(lp[0] * lp[1])) - jnp.exp(jnp.sum(lp[2] * lp[3])) + lam_init
        lam_arr = jnp.concatenate([jnp.full((1, LANE), 1.0, F32) * lam,
                                   jnp.full((7, LANE), 1.0 - lam_init, F32)], 0)
        yd = _flash(Ga, Ga, GvT, npairs=DIFF_HEADS, t=t_att, q_blk=(GB_DQ, 1), k_blk=(GB_DK, 1), v_blk=(0, 1),
                    bias=bias_diff, bias_shared=True, diff=(lam_arr, diff_subln[l][None]))

        wbr = jnp.stack([w_branch[l, 0], _pair_rows(w_branch[l, 1]), _pair_rows(w_branch[l, 2]),
                         w_branch[l, 3]]).astype(BF16)
        f2 = lambda a: a.reshape(T, a.shape[-1])
        h = _merge(h, gl[1], f2(ya), f2(yb), f2(oc), f2(osel), f2(ow), f2(Fa), f2(yd),
                   w_gate[l].astype(BF16), wbr, w_o[l].astype(BF16), p3)

        h = _ffn(h, gl[2], ffn_w_gate[l, 1].astype(BF16), ffn_w_up[l, 1].astype(BF16),
                 ffn_w_down[l, 1].astype(BF16), final_g[None], final=(l == depth - 1))
    return h.reshape(B, S, D)
```

```python
import functools
import math

import numpy as np
import jax
import jax.numpy as jnp
from jax import lax
from jax.experimental import pallas as pl
from jax.experimental.pallas import tpu as pltpu

F32 = jnp.float32
BF16 = jnp.bfloat16

D_MODEL = 1024
EPS = 1e-6
NEG = -1e30
D_FF = 2816

MLA_HEADS = 8
MLA_Q_RANK = 256
MLA_KV_RANK = 128
MLA_NOPE = 64
MLA_ROPE = 32
MLA_V = 64
ROPE_THETA = 10000.0

SWA_HEADS = 8
SWA_WINDOW = 128

NSA_HEADS = 8
NSA_HD = 64
NSA_CMP_LEN = 32
NSA_CMP_STRIDE = 16
NSA_CMP_HIDDEN = 128
NSA_SEL_LEN = 64
NSA_TOPK = 16
NSA_WINDOW = 512
NSA_FORCE = 1e4

DIFF_HEADS = 4
DIFF_HD = 64

N_BUCKETS = 32
MAX_DIST = 128

LANE = 128
HALF = 64
VMEM_LIMIT = 56 * 1024 * 1024
LOG2E = math.log2(math.e)
KEY_CHUNK = 256

O_CQ, O_CKV, O_KROPE, O_SQ, O_SK, O_SV, O_NQ = 0, 256, 384, 416, 928, 1056, 1184
O_NKC, O_NVC, O_NKS, O_NVS, O_NKW, O_NVW, O_NGATE = 1696, 1824, 1952, 2080, 2208, 2336, 2464
O_DQ, O_DK, O_DV = 2488, 3000, 3512

F_COLS = 1024
FB_CKV, FB_KROPE, FB_KSWAP, FB_NKC, FB_NVC, FB_NGATE = 2, 3, 4, 5, 6, 7
G_COLS = 3328
GB_SQ, GB_NQ, GB_DQ, GB_DK, GB_DV = 0, 4, 8, 12, 16
GB_SK, GB_SV, GB_NKS, GB_NVS, GB_NKW, GB_NVW = 20, 21, 22, 23, 24, 25


def _cparams(sem):
    return pltpu.CompilerParams(dimension_semantics=sem, vmem_limit_bytes=VMEM_LIMIT)


def _rms(x, g):
    return x * lax.rsqrt(jnp.mean(x * x, -1, keepdims=True) + EPS) * g


def _dot(a, b):
    return jnp.dot(a, b, preferred_element_type=F32)


def _dot_nt(a, b):
    return lax.dot_general(a, b, (((1,), (1,)), ((), ())), preferred_element_type=F32)


def _split3(x):
    hi = x.astype(BF16)
    r = x - hi.astype(F32)
    mid = r.astype(BF16)
    lo = (r - mid.astype(F32)).astype(BF16)
    return hi, mid, lo


def _dot_exact(x, w01):
    hi, mid, lo = _split3(x)
    return _dot(hi, w01) + _dot(mid, w01) + _dot(lo, w01)


def _ffn_kernel(h_ref, g_ref, wg_ref, wu_ref, wd_ref, gf_ref, o_ref, n_scr, acc_scr, *, final):
    j = pl.program_id(1)

    @pl.when(j == 0)
    def _():
        n_scr[...] = _rms(h_ref[...], g_ref[...]).astype(BF16)
        acc_scr[...] = jnp.zeros_like(acc_scr)

    n = n_scr[...]
    a = _dot(n, wg_ref[...])
    b = _dot(n, wu_ref[...])
    t = (a * jax.nn.sigmoid(a) * b).astype(BF16)
    acc_scr[...] += _dot(t, wd_ref[...])

    @pl.when(j == pl.num_programs(1) - 1)
    def _():
        out = h_ref[...] + 0.5 * acc_scr[...]
        if final:
            out = _rms(out, gf_ref[...])
        o_ref[...] = out


def _ffn(h, g, wg, wu, wd, gf, final):
    T = h.shape[0]
    tm, tf = 512, D_FF // 2
    return pl.pallas_call(
        functools.partial(_ffn_kernel, final=final),
        grid=(T // tm, D_FF // tf),
        in_specs=[
            pl.BlockSpec((tm, D_MODEL), lambda i, j: (i, 0)),
            pl.BlockSpec((1, D_MODEL), lambda i, j: (0, 0)),
            pl.BlockSpec((D_MODEL, tf), lambda i, j: (0, j)),
            pl.BlockSpec((D_MODEL, tf), lambda i, j: (0, j)),
            pl.BlockSpec((tf, D_MODEL), lambda i, j: (j, 0)),
            pl.BlockSpec((1, D_MODEL), lambda i, j: (0, 0)),
        ],
        out_specs=pl.BlockSpec((tm, D_MODEL), lambda i, j: (i, 0)),
        out_shape=jax.ShapeDtypeStruct((T, D_MODEL), F32),
        scratch_shapes=[pltpu.VMEM((tm, D_MODEL), BF16), pltpu.VMEM((tm, D_MODEL), F32)],
        compiler_params=_cparams(("parallel", "arbitrary")),
        name="ffn_half_step",
    )(h, g, wg, wu, wd, gf)


def _proj_kernel(h_ref, g_ref, w_ref, f_ref, gq_ref):
    u = _rms(h_ref[...], g_ref[...]).astype(BF16)
    f_ref[...] = _dot(u, w_ref[:, :F_COLS])
    step = 512
    for c0 in range(0, G_COLS, step):
        c1 = min(c0 + step, G_COLS)
        gq_ref[:, c0:c1] = _dot(u, w_ref[:, F_COLS + c0:F_COLS + c1]).astype(BF16)


def _proj(h, g, wp):
    T = h.shape[0]
    tm = 512
    return pl.pallas_call(
        _proj_kernel,
        grid=(T // tm,),
        in_specs=[
            pl.BlockSpec((tm, D_MODEL), lambda i: (i, 0)),
            pl.BlockSpec((1, D_MODEL), lambda i: (0, 0)),
            pl.BlockSpec((D_MODEL, F_COLS + G_COLS), lambda i: (0, 0)),
        ],
        out_specs=[
            pl.BlockSpec((tm, F_COLS), lambda i: (i, 0)),
            pl.BlockSpec((tm, G_COLS), lambda i: (i, 0)),
        ],
        out_shape=[jax.ShapeDtypeStruct((T, F_COLS), F32),
                   jax.ShapeDtypeStruct((T, G_COLS), BF16)],
        compiler_params=_cparams(("parallel",)),
        name="in_proj",
    )(h, g, wp)


def _proj_weight(w_in):
    ncol = F_COLS + G_COLS
    src = np.zeros((ncol,), np.int32)
    scale = np.zeros((ncol,), np.float32)

    def put(dst, srcs, s=1.0):
        srcs = np.asarray(srcs)
        src[dst:dst + len(srcs)] = srcs
        scale[dst:dst + len(srcs)] = s

    half = MLA_ROPE // 2
    put(0, O_CQ + np.arange(MLA_Q_RANK))
    put(FB_CKV * LANE, O_CKV + np.arange(MLA_KV_RANK))
    put(FB_KROPE * LANE + MLA_NOPE, O_KROPE + np.arange(MLA_ROPE))
    put(FB_KSWAP * LANE + MLA_NOPE, O_KROPE + np.concatenate([np.arange(half, MLA_ROPE), np.arange(half)]))
    put(FB_NKC * LANE, O_NKC + np.arange(LANE))
    put(FB_NVC * LANE, O_NVC + np.arange(LANE))
    put(FB_NGATE * LANE, O_NGATE + np.arange(3 * NSA_HEADS))
    g0 = F_COLS
    qscale = 0.125 * LOG2E
    pair_perm = np.concatenate([np.concatenate([(j) * HALF + np.arange(HALF), (4 + j) * HALF + np.arange(HALF)])
                                for j in range(4)])
    put(g0 + GB_SQ * LANE, O_SQ + pair_perm, qscale)
    put(g0 + GB_NQ * LANE, O_NQ + pair_perm, qscale)
    put(g0 + GB_DQ * LANE, O_DQ + np.arange(512), qscale)
    put(g0 + GB_DK * LANE, O_DK + np.arange(512))
    put(g0 + GB_DV * LANE, O_DV + np.arange(512))
    for blk, off in ((GB_SK, O_SK), (GB_SV, O_SV), (GB_NKS, O_NKS), (GB_NVS, O_NVS),
                     (GB_NKW, O_NKW), (GB_NVW, O_NVW)):
        put(g0 + blk * LANE, off + np.arange(LANE))
    return (w_in[:, src] * scale[None, :]).astype(BF16)


PAIR_HEAD = np.array([[j, 4 + j] for j in range(4)])


def _mla_prep_kernel(cq_ref, ckv_ref, kr_ref, ks_ref, ct_ref, st_ref, qg_ref, kg_ref,
                     wa_ref, wb_ref, wk_ref, wv_ref, q_ref, k_ref, v_ref, *, scale):
    qn = _rms(cq_ref[0], qg_ref[...]).astype(BF16)
    kn = _rms(ckv_ref[0], kg_ref[...]).astype(BF16)
    ct = ct_ref[...]
    st = st_ref[...]
    krot = kr_ref[0] * ct + ks_ref[0] * st
    v_ref[0] = _dot(kn, wv_ref[...]).astype(BF16)
    for h in range(MLA_HEADS):
        sl = slice(h * LANE, (h + 1) * LANE)
        qa = _dot(qn, wa_ref[:, sl])
        qb = _dot(qn, wb_ref[:, sl])
        q_ref[0, :, sl] = ((qa * ct + qb * st) * scale).astype(BF16)
        k_ref[0, :, sl] = (_dot(kn, wk_ref[:, sl]) + krot).astype(BF16)


def _mla_prep(Fa, ctab, stab, qg, kg, wa, wb, wk, wv):
    B, S, _ = Fa.shape
    tm = 512
    HW = MLA_HEADS * LANE
    full = lambda shape: pl.BlockSpec(shape, lambda b, i: (0,) * len(shape))
    return pl.pallas_call(
        functools.partial(_mla_prep_kernel, scale=(MLA_NOPE + MLA_ROPE) ** -0.5 * LOG2E),
        grid=(B, S // tm),
        in_specs=[
            pl.BlockSpec((1, tm, MLA_Q_RANK), lambda b, i: (b, i, 0)),
            pl.BlockSpec((1, tm, LANE), lambda b, i: (b, i, FB_CKV)),
            pl.BlockSpec((1, tm, LANE), lambda b, i: (b, i, FB_KROPE)),
            pl.BlockSpec((1, tm, LANE), lambda b, i: (b, i, FB_KSWAP)),
            pl.BlockSpec((tm, LANE), lambda b, i: (i, 0)),
            pl.BlockSpec((tm, LANE), lambda b, i: (i, 0)),
            full((1, MLA_Q_RANK)), full((1, MLA_KV_RANK)),
            full((MLA_Q_RANK, HW)), full((MLA_Q_RANK, HW)),
            full((MLA_KV_RANK, HW)), full((MLA_KV_RANK, MLA_HEADS * MLA_V)),
        ],
        out_specs=[
            pl.BlockSpec((1, tm, HW), lambda b, i: (b, i, 0)),
            pl.BlockSpec((1, tm, HW), lambda b, i: (b, i, 0)),
            pl.BlockSpec((1, tm, MLA_HEADS * MLA_V), lambda b, i: (b, i, 0)),
        ],
        out_shape=[jax.ShapeDtypeStruct((B, S, HW), BF16),
                   jax.ShapeDtypeStruct((B, S, HW), BF16),
                   jax.ShapeDtypeStruct((B, S, MLA_HEADS * MLA_V), BF16)],
        compiler_params=_cparams(("parallel", "parallel")),
        name="mla_prep",
    )(Fa, Fa, Fa, Fa, ctab, stab, qg, kg, wa, wb, wk, wv)


def _mla_weights(w_uq, w_ukv):
    dq = MLA_NOPE + MLA_ROPE
    half = MLA_ROPE // 2
    z = lambda n: jnp.zeros((w_uq.shape[0], n), F32)
    wa, wb, wk, wv = [], [], [], []
    for h in range(MLA_HEADS):
        nope = w_uq[:, h * dq:h * dq + MLA_NOPE]
        r = w_uq[:, h * dq + MLA_NOPE:(h + 1) * dq]
        wa += [nope, r, z(LANE - dq)]
        wb += [z(MLA_NOPE), r[:, half:], r[:, :half], z(LANE - dq)]
        wk += [w_ukv[:, h * LANE:h * LANE + MLA_NOPE], jnp.zeros((w_ukv.shape[0], LANE - MLA_NOPE), F32)]
        wv += [w_ukv[:, h * LANE + MLA_NOPE:(h + 1) * LANE]]
    cat = lambda xs: jnp.concatenate(xs, axis=1).astype(BF16)
    return cat(wa), cat(wb), cat(wk), cat(wv)


def _rope_tables(S):
    half = MLA_ROPE // 2
    freqs = ROPE_THETA ** (-jnp.arange(half, dtype=F32) / half)
    ang = jnp.arange(S, dtype=jnp.int32)[:, None].astype(F32) * freqs
    cos, sin = jnp.cos(ang), jnp.sin(ang)
    ones = jnp.ones((S, MLA_NOPE), F32)
    zeros = jnp.zeros((S, LANE - MLA_NOPE - MLA_ROPE), F32)
    ctab = jnp.concatenate([ones, cos, cos, zeros], 1)
    stab = jnp.concatenate([jnp.zeros((S, MLA_NOPE), F32), -sin, sin, zeros], 1)
    return ctab, stab


def _t5_bucket(dist):
    n = jnp.maximum(dist, 0)
    max_exact = N_BUCKETS // 2
    nf = jnp.maximum(n, 1).astype(F32)
    large = max_exact + (jnp.log(nf / max_exact) / math.log(MAX_DIST / max_exact)
                         * (N_BUCKETS - max_exact)).astype(jnp.int32)
    large = jnp.minimum(large, N_BUCKETS - 1)
    return jnp.where(n < max_exact, n, large)


def _bias_tiles(tab, t):
    assert t >= MAX_DIST
    i = jnp.arange(t, dtype=jnp.int32)[:, None]
    j = jnp.arange(t, dtype=jnp.int32)[None, :]
    tiles = []
    tab_t = tab.T[:, :, None, None]
    for rel in range(2):
        b = _t5_bucket(rel * t + j - i)[None]
        tile = jnp.zeros((tab.shape[1], t, t), F32)
        for n in range(N_BUCKETS):
            tile = jnp.where(b == n, tab_t[:, n], tile)
        tiles.append(tile)
    far = tab[N_BUCKETS - 1][:, None, None, None]
    return (jnp.stack(tiles, 1) - far) * LOG2E


def _flash_kernel(*refs, t, kc, window, shared_qk, has_bias, bias_shared, has_sel, has_sink, diff, out_dtype):
    qt_ref, kt_ref, first_ref, last_ref, rel_ref = refs[:5]
    refs = list(refs[5:])
    if shared_qk:
        q_ref, k_ref = refs[:2]
        refs = refs[2:]
    else:
        qa_ref, qb_ref, ka_ref, kb_ref = refs[:4]
        refs = refs[4:]
    vt_ref = refs.pop(0)
    if has_bias:
        ba_ref = refs.pop(0)
        bb_ref = ba_ref if bias_shared else refs.pop(0)
    if has_sel:
        sa_ref, sb_ref, et_ref = refs[:3]
        refs = refs[3:]
    if has_sink:
        sink_ref = refs.pop(0)
    if diff:
        lam_ref, sub_ref = refs[:2]
        refs = refs[2:]
    o_ref, m_scr, l_scr, acc_scr, p_scr = refs

    step = pl.program_id(2)
    qt = qt_ref[step]
    kt = kt_ref[step]
    lo = lax.broadcasted_iota(jnp.int32, (1, LANE), 1) < HALF

    @pl.when(first_ref[step] == 1)
    def _():
        for a in range(2):
            if has_sink:
                m_scr[a] = jnp.broadcast_to(sink_ref[0, a:a + 1, 0:1], (1, t))
                l_scr[a] = jnp.ones((1, t), F32)
            else:
                m_scr[a] = jnp.full((1, t), NEG, F32)
                l_scr[a] = jnp.zeros((1, t), F32)
            acc_scr[a] = jnp.zeros((LANE, t), F32)

    if shared_qk:
        q = q_ref[0]
        zero = jnp.zeros_like(q)
        qs = (jnp.where(lo, q, zero), jnp.where(lo, zero, q))
        ks = (k_ref[0], k_ref[0])
    else:
        qs = (qa_ref[0], qb_ref[0])
        ks = (ka_ref[0], kb_ref[0])
    vt = vt_ref[0]

    def process(rel):
        col_slices = [slice(c, c + LANE) for c in range(0, t, LANE)]

        def unit_range(kh, c):
            base = rel * t + c - kh
            return base, base - (kc - 1), base + (LANE - 1)

        def unit_dead(kh, c):
            _, dmin, dmax = unit_range(kh, c)
            return (rel == 0 and dmax < 0) or (window is not None and dmin >= window)

        jobs = [(kh, a) for kh in range(0, t, kc) for a in range(2)
                if not all(unit_dead(kh, cs.start) for cs in col_slices)]

        def scores(job):
            kh, a = job
            kk, qq = ks[a][kh:kh + kc], qs[a]
            if has_sel:
                kk = jnp.concatenate([kk, et_ref[kh:kh + kc, :]], axis=1)
                qq = jnp.concatenate([qq, (sa_ref if a == 0 else sb_ref)[0]], axis=1)
            return _dot_nt(kk, qq)

        m = [[m_scr[a, :, cs] for cs in col_slices] for a in range(2)]
        l = [[l_scr[a, :, cs] for cs in col_slices] for a in range(2)]
        acc = [acc_scr[0], acc_scr[1]]
        nxt = scores(jobs[0])
        for ji, (kh, a) in enumerate(jobs):
            keys = slice(kh, kh + kc)
            st = nxt
            if ji + 1 < len(jobs):
                nxt = scores(jobs[ji + 1])
            b_ref = (ba_ref if a == 0 else bb_ref) if has_bias else None
            alphas = []
            for ci, cols in enumerate(col_slices):
                base, dmin, dmax = unit_range(kh, cols.start)
                if unit_dead(kh, cols.start):
                    alphas.append(jnp.ones((1, LANE), F32))
                    p_scr[a, keys, cols] = jnp.zeros((kc, LANE), BF16)
                    continue
                s = st[:, cols]
                if has_bias and rel < 2 and dmin < MAX_DIST:
                    s = s + b_ref[0, 0, 0, keys, cols]
                need_lo = rel == 0 and dmin < 0
                need_hi = window is not None and dmax >= window
                if need_lo or need_hi:
                    d = (lax.broadcasted_iota(jnp.int32, (kc, LANE), 1)
                         - lax.broadcasted_iota(jnp.int32, (kc, LANE), 0)) + base
                    valid = (d >= 0) if need_lo else None
                    if need_hi:
                        valid = (d < window) if valid is None else (valid & (d < window))
                    s = jnp.where(valid, s, NEG)
                m_prev = m[a][ci]
                m_new = jnp.maximum(m_prev, jnp.max(s, 0, keepdims=True))
                alpha = jnp.exp2(m_prev - m_new)
                p = jnp.exp2(s - m_new)
                l[a][ci] = alpha * l[a][ci] + jnp.sum(p, 0, keepdims=True)
                m[a][ci] = m_new
                alphas.append(alpha)
                p_scr[a, keys, cols] = p.astype(BF16)
            acc[a] = jnp.concatenate(alphas, axis=1) * acc[a] + _dot(vt[:, keys], p_scr[a, keys, :])
        for a in range(2):
            for ci, cs in enumerate(col_slices):
                m_scr[a, :, cs] = m[a][ci]
                l_scr[a, :, cs] = l[a][ci]
            acc_scr[a] = acc[a]

    for r in range(2 if window is not None else 3):
        @pl.when(rel_ref[step] == r)
        def _(r=r):
            process(r)

    @pl.when(last_ref[step] == 1)
    def _():
        o0 = acc_scr[0] / l_scr[0]
        o1 = acc_scr[1] / l_scr[1]
        if diff:
            o = (o0 - lam_ref[0:1, 0:1] * o1).T
            o = _rms(o, sub_ref[...]) * lam_ref[1:2, 0:1]
        else:
            feat = lax.broadcasted_iota(jnp.int32, (LANE, 1), 0)
            o = jnp.where(feat < HALF, o0, o1).T
        o_ref[0] = o.astype(out_dtype)


def _flash(q_arr, k_arr, vt_arr, *, npairs, t, q_blk, k_blk, v_blk, window=None, shared_qk=True,
           bias=None, bias_shared=False, sel=None, emat=None, sinks=None, diff=None, out_dtype=BF16):
    B, S, _ = q_arr.shape
    nq = S // t
    nprev = None if window is None else -(-(window - 1) // t)
    assert window is None or window <= t + 1, "a window must end in the previous tile"
    steps = []
    for qi in range(nq):
        k0 = 0 if nprev is None else max(0, qi - nprev)
        for ki in range(k0, qi + 1):
            steps.append((qi, ki, int(ki == k0), int(ki == qi), min(qi - ki, 2)))
    tabs = [jnp.asarray(np.array([s[c] for s in steps], np.int32)) for c in range(5)]
    nsteps = len(steps)
    kc = min(t, KEY_CHUNK)

    def qmap(off, stride):
        return lambda b, p, s, qt, kt, fr, la, rl: (b, qt[s], off + stride * p)

    def kmap(off, stride):
        return lambda b, p, s, qt, kt, fr, la, rl: (b, kt[s], off + stride * p)

    in_specs, args = [], []
    if shared_qk:
        in_specs += [pl.BlockSpec((1, t, LANE), qmap(*q_blk)), pl.BlockSpec((1, t, LANE), kmap(*k_blk))]
        args += [q_arr, k_arr]
    else:
        in_specs += [pl.BlockSpec((1, t, LANE), qmap(*q_blk[0])), pl.BlockSpec((1, t, LANE), qmap(*q_blk[1])),
                     pl.BlockSpec((1, t, LANE), kmap(*k_blk[0])), pl.BlockSpec((1, t, LANE), kmap(*k_blk[1]))]
        args += [q_arr, q_arr, k_arr, k_arr]
    voff, vstride = v_blk
    in_specs.append(pl.BlockSpec((1, LANE, t),
                                 lambda b, p, s, qt, kt, fr, la, rl: (b, voff + vstride * p, kt[s])))
    args.append(vt_arr)
    if bias is not None:
        for a in range(1 if bias_shared else 2):
            in_specs.append(pl.BlockSpec((1, 1, 1, t, t),
                                         lambda b, p, s, qt, kt, fr, la, rl, a=a: (a, p, jnp.minimum(rl[s], 1), 0, 0)))
            args.append(bias)
    if sel is not None:
        nselp = emat.shape[1]
        for a in range(2):
            in_specs.append(pl.BlockSpec((1, t, nselp),
                                         lambda b, p, s, qt, kt, fr, la, rl, a=a: (b, qt[s], a)))
            args.append(sel)
        in_specs.append(pl.BlockSpec((t, nselp), lambda b, p, s, qt, kt, fr, la, rl: (kt[s], 0)))
        args.append(emat)
    if sinks is not None:
        in_specs.append(pl.BlockSpec((1, 8, LANE), lambda b, p, s, qt, kt, fr, la, rl: (p, 0, 0)))
        args.append(sinks)
    if diff is not None:
        lam_arr, sub = diff
        in_specs.append(pl.BlockSpec((8, LANE), lambda b, p, s, qt, kt, fr, la, rl: (0, 0)))
        in_specs.append(pl.BlockSpec((1, LANE), lambda b, p, s, qt, kt, fr, la, rl: (0, 0)))
        args += [lam_arr, sub]

    kern = functools.partial(
        _flash_kernel, t=t, kc=kc, window=window, shared_qk=shared_qk, has_bias=bias is not None,
        bias_shared=bias_shared, has_sel=sel is not None, has_sink=sinks is not None,
        diff=diff is not None, out_dtype=out_dtype)
    return pl.pallas_call(
        kern,
        grid_spec=pltpu.PrefetchScalarGridSpec(
            num_scalar_prefetch=5,
            grid=(B, npairs, nsteps),
            in_specs=in_specs,
            out_specs=pl.BlockSpec((1, t, LANE), lambda b, p, s, qt, kt, fr, la, rl: (b, qt[s], p)),
            scratch_shapes=[pltpu.VMEM((2, 1, t), F32), pltpu.VMEM((2, 1, t), F32),
                            pltpu.VMEM((2, LANE, t), F32), pltpu.VMEM((2, t, t), BF16)],
        ),
        out_shape=jax.ShapeDtypeStruct((B, S, npairs * LANE), out_dtype),
        compiler_params=_cparams(("parallel", "parallel", "arbitrary")),
        name="flash_pairs",
    )(*tabs, *args)


def _compress_kernel(r_ref, pt_ref, pb_ref, w1_ref, w2_ref, o_ref):
    r = r_ref[0, 0]
    n = r.shape[0]
    xt = (r + pt_ref[0]).astype(BF16)
    xb = (r + pb_ref[0]).astype(BF16)
    out = jnp.zeros((n, LANE), F32)
    for h in range(2):
        top = _dot(xt, w1_ref[0, h, 0])
        bot = _dot(xb, w1_ref[0, h, 1])
        x = top + pltpu.roll(bot, n - 1, 0)
        hd = 0.5 * x * (1.0 + jnp.tanh(math.sqrt(2.0 / math.pi) * (x + 0.044715 * (x * x * x))))
        out = out + _dot(hd.astype(BF16), w2_ref[0, h])
    o_ref[0, 0] = out.astype(BF16)


def _compress(r2, ptop, pbot, w1e, w2e):
    _, B, n, K = r2.shape
    return pl.pallas_call(
        _compress_kernel,
        grid=(2, B),
        in_specs=[
            pl.BlockSpec((1, 1, n, K), lambda c, b: (c, b, 0, 0)),
            pl.BlockSpec((1, 1, K), lambda c, b: (c, 0, 0)),
            pl.BlockSpec((1, 1, K), lambda c, b: (c, 0, 0)),
            pl.BlockSpec((1, 2, 2, K, NSA_CMP_HIDDEN), lambda c, b: (c, 0, 0, 0, 0)),
            pl.BlockSpec((1, 2, NSA_CMP_HIDDEN, LANE), lambda c, b: (c, 0, 0, 0)),
        ],
        out_specs=pl.BlockSpec((1, 1, n, LANE), lambda c, b: (c, b, 0, 0)),
        out_shape=jax.ShapeDtypeStruct((2, B, n, LANE), BF16),
        compiler_params=_cparams(("parallel", "parallel")),
        name="nsa_compress",
    )(r2, ptop, pbot, w1e, w2e)


def _compress_weights(cmp_pos, cmp_w1, cmp_w2):
    L2 = NSA_CMP_STRIDE
    d = NSA_HD
    w1 = cmp_w1.reshape(2, 2, L2, d, NSA_CMP_HIDDEN)
    z = jnp.zeros_like(w1)
    e0 = jnp.concatenate([w1, z], axis=3)
    e1 = jnp.concatenate([z, w1], axis=3)
    w1e = jnp.stack([e0, e1], axis=1).reshape(2, 2, 2, L2 * 2 * d, NSA_CMP_HIDDEN).astype(BF16)
    pos = cmp_pos.reshape(2, 2, L2, 1, d)
    pos = jnp.broadcast_to(pos, (2, 2, L2, 2, d)).reshape(2, 2, 1, L2 * 2 * d)
    zw = jnp.zeros_like(cmp_w2)
    w2e = jnp.stack([jnp.concatenate([cmp_w2, zw], -1), jnp.concatenate([zw, cmp_w2], -1)], 1).astype(BF16)
    return pos[:, 0], pos[:, 1], w1e, w2e


def _cmp_attn_kernel(q_ref, kc_ref, vc_ref, ov_ref, oc_ref, sel_ref, *, tq, ncp, nselp):
    i = pl.program_id(1)
    qpos = i * tq + lax.broadcasted_iota(jnp.int32, (tq, 1), 0)
    cidx = lax.broadcasted_iota(jnp.int32, (1, ncp), 1)
    cvalid = (cidx * NSA_CMP_STRIDE + (NSA_CMP_LEN - 1)) <= qpos
    lane = lax.broadcasted_iota(jnp.int32, (1, LANE), 1)
    lo = lane < HALF
    kc = kc_ref[0, 0]
    vc = vc_ref[0, 0]
    psum = [jnp.zeros((tq, ncp), F32), jnp.zeros((tq, ncp), F32)]
    for j in range(4):
        q2 = q_ref[0, :, j * LANE:(j + 1) * LANE]
        zero = jnp.zeros_like(q2)
        outs = []
        for a in range(2):
            qa = jnp.where(lo, q2, zero) if a == 0 else jnp.where(lo, zero, q2)
            s = jnp.where(cvalid, _dot_nt(qa, kc), NEG)
            m = jnp.max(s, -1, keepdims=True)
            e = jnp.where(cvalid, jnp.exp2(s - m), 0.0)
            den = jnp.sum(e, -1, keepdims=True)
            p = e / jnp.where(den > 0, den, 1.0)
            psum[a] = psum[a] + p
            outs.append(_dot(p.astype(BF16), vc))
        oc_ref[0, :, j * LANE:(j + 1) * LANE] = jnp.where(lo, outs[0], outs[1])

    n = lax.broadcasted_iota(jnp.int32, (1, nselp), 1)
    cur = qpos >> int(math.log2(NSA_SEL_LEN))
    causal = (n * NSA_SEL_LEN) <= qpos
    forced = (n == 0) | (n == cur) | (n == cur - 1)
    nf = n.astype(F32)
    for a in range(2):
        imp = _dot_exact(psum[a], ov_ref[...])
        x = jnp.where(causal, jnp.where(forced, NSA_FORCE, imp), -1.0)
        picked = jnp.zeros((tq, nselp), F32)
        for _ in range(NSA_TOPK):
            mx = jnp.max(x, -1, keepdims=True)
            idx = jnp.min(jnp.where(x == mx, nf, float(nselp)), -1, keepdims=True)
            hit = nf == idx
            picked = jnp.where(hit, 1.0, picked)
            x = jnp.where(hit, -2.0, x)
        sel_ref[0, :, a * nselp:(a + 1) * nselp] = jnp.where(causal & (picked > 0.5), 0.0, NEG).astype(BF16)


def _cmp_attn(G3, kv2, ovl):
    B, S, _ = G3.shape
    ncp = kv2.shape[2]
    nselp = ovl.shape[1]
    tq = 256
    return pl.pallas_call(
        functools.partial(_cmp_attn_kernel, tq=tq, ncp=ncp, nselp=nselp),
        grid=(B, S // tq),
        in_specs=[
            pl.BlockSpec((1, tq, 4 * LANE), lambda b, i: (b, i, GB_NQ // 4)),
            pl.BlockSpec((1, 1, ncp, LANE), lambda b, i: (0, b, 0, 0)),
            pl.BlockSpec((1, 1, ncp, LANE), lambda b, i: (1, b, 0, 0)),
            pl.BlockSpec((ncp, nselp), lambda b, i: (0, 0)),
        ],
        out_specs=[
            pl.BlockSpec((1, tq, 4 * LANE), lambda b, i: (b, i, 0)),
            pl.BlockSpec((1, tq, 2 * nselp), lambda b, i: (b, i, 0)),
        ],
        out_shape=[jax.ShapeDtypeStruct((B, S, 4 * LANE), F32),
                   jax.ShapeDtypeStruct((B, S, 2 * nselp), BF16)],
        compiler_params=_cparams(("parallel", "parallel")),
        name="nsa_cmp_attn_topk",
    )(G3, kv2, kv2, ovl)


def _merge_kernel(h_ref, g_ref, ya_ref, yb_ref, oc_ref, os_ref, ow_ref, ng_ref, yd_ref,
                  wg_ref, wb_ref, wo_ref, p3_ref, o_ref):
    h = h_ref[...]
    u = _rms(h, g_ref[...]).astype(BF16)
    gexp = _dot_exact(jax.nn.sigmoid(ng_ref[...]), p3_ref[...])
    W = 4 * LANE
    yc = gexp[:, :W] * oc_ref[...] + gexp[:, W:2 * W] * os_ref[...] + gexp[:, 2 * W:] * ow_ref[...]
    ys = (ya_ref[...], yb_ref[...], yc.astype(BF16), yd_ref[...])
    merged = jnp.zeros(h.shape, F32)
    for i in range(4):
        gate = jax.nn.sigmoid(_dot(u, wg_ref[i]))
        merged = merged + gate * _dot(ys[i], wb_ref[i])
    o_ref[...] = h + _dot(merged.astype(BF16), wo_ref[...])


def _merge(h, g, ya, yb, oc, osel, ow, Fa, yd, wg, wb, wo, p3):
    T = h.shape[0]
    tm = 256
    W = 4 * LANE
    row = lambda w: pl.BlockSpec((tm, w), lambda i: (i, 0))
    const = lambda shape: pl.BlockSpec(shape, lambda i: (0,) * len(shape))
    return pl.pallas_call(
        _merge_kernel,
        grid=(T // tm,),
        in_specs=[
            row(D_MODEL), const((1, D_MODEL)), row(W), row(W), row(W), row(W), row(W),
            pl.BlockSpec((tm, LANE), lambda i: (i, FB_NGATE)), row(W),
            const((4, D_MODEL, D_MODEL)), const((4, W, D_MODEL)), const((D_MODEL, D_MODEL)),
            const((LANE, 3 * W)),
        ],
        out_specs=row(D_MODEL),
        out_shape=jax.ShapeDtypeStruct((T, D_MODEL), F32),
        compiler_params=_cparams(("parallel",)),
        name="gated_merge",
    )(h, g, ya, yb, oc, osel, ow, Fa, yd, wg, wb, wo, p3)


def _gate_expand_matrix():
    W = 4 * LANE
    m = np.zeros((LANE, 3 * W), np.float32)
    for j in range(4):
        for a in range(2):
            head = PAIR_HEAD[j, a]
            for r in range(3):
                c0 = r * W + j * LANE + a * HALF
                m[head * 3 + r, c0:c0 + HALF] = 1.0
    return jnp.asarray(m, BF16)


def _pair_rows(w):
    perm = np.concatenate([np.arange(HALF) + PAIR_HEAD[j, a] * HALF for j in range(4) for a in range(2)])
    return w[perm]


def kernel(x, norm_g, w_in, mla_q_norm, mla_kv_norm, mla_w_uq, mla_w_ukv, swa_sinks,
           nsa_cmp_pos, nsa_cmp_w1, nsa_cmp_w2, diff_lambda, diff_subln, rel_bias_table,
           w_branch, w_gate, w_o, ffn_w_gate, ffn_w_up, ffn_w_down, final_g):
    B, S, D = x.shape
    T = B * S
    depth = w_in.shape[0]
    t_att = 512
    ncp = S // NSA_CMP_STRIDE
    nsel = S // NSA_SEL_LEN
    nselp = -(-nsel // LANE) * LANE

    tab_swa = rel_bias_table[:, :SWA_HEADS]
    tab_nsa = rel_bias_table[:, SWA_HEADS:SWA_HEADS + NSA_HEADS]
    tab_diff = rel_bias_table[:, SWA_HEADS + NSA_HEADS:]
    pair_heads = PAIR_HEAD.T.reshape(-1)
    bias_swa = _bias_tiles(tab_swa[:, pair_heads], t_att).reshape(2, 4, 2, t_att, t_att)
    bias_nsa = _bias_tiles(tab_nsa[:, pair_heads], t_att).reshape(2, 4, 2, t_att, t_att)
    bias_diff = _bias_tiles(tab_diff, t_att)[None]
    ctab, stab = _rope_tables(S)

    ci = np.arange(ncp)[:, None]
    ni = np.arange(nselp)[None, :]
    nc_real = (S - NSA_CMP_LEN) // NSA_CMP_STRIDE + 1
    ovl = ((ci * NSA_CMP_STRIDE < ni * NSA_SEL_LEN + NSA_SEL_LEN)
           & (ni * NSA_SEL_LEN < ci * NSA_CMP_STRIDE + NSA_CMP_LEN) & (ci < nc_real) & (ni < nsel))
    ovl = jnp.asarray(ovl.astype(np.float32), BF16)
    emat = jnp.asarray((np.arange(S)[:, None] // NSA_SEL_LEN == np.arange(nselp)[None, :]).astype(np.float32), BF16)
    p3 = _gate_expand_matrix()

    h = x.reshape(T, D)
    for l in range(depth):
        gl = norm_g[l].reshape(3, 1, D)
        h = _ffn(h, gl[0], ffn_w_gate[l, 0].astype(BF16), ffn_w_up[l, 0].astype(BF16),
                 ffn_w_down[l, 0].astype(BF16), gl[0], final=False)

        Fa, Ga = _proj(h, gl[1], _proj_weight(w_in[l]))
        Fa = Fa.reshape(B, S, F_COLS)
        Ga = Ga.reshape(B, S, G_COLS)

        wa, wb_, wk, wv = _mla_weights(mla_w_uq[l], mla_w_ukv[l])
        Q, K, V = _mla_prep(Fa, ctab, stab, mla_q_norm[l][None], mla_kv_norm[l][None], wa, wb_, wk, wv)
        ya = _flash(Q, K, V.transpose(0, 2, 1), npairs=4, t=t_att, shared_qk=False,
                    q_blk=((0, 2), (1, 2)), k_blk=((0, 2), (1, 2)), v_blk=(0, 1))

        sink = ((swa_sinks[l] - tab_swa[N_BUCKETS - 1]) * LOG2E)[PAIR_HEAD]
        sink = jnp.broadcast_to(jnp.pad(sink, ((0, 0), (0, 6)))[:, :, None], (4, 8, LANE)).astype(F32)
        GvT = Ga[:, :, GB_DV * LANE:].transpose(0, 2, 1)
        yb = _flash(Ga, Ga, GvT, npairs=4, t=t_att, window=SWA_WINDOW,
                    q_blk=(GB_SQ, 1), k_blk=(GB_SK, 0), v_blk=(GB_SV - GB_DV, 0), bias=bias_swa, sinks=sink)

        r2 = jnp.stack([Fa[:, :, FB_NKC * LANE:(FB_NKC + 1) * LANE],
                        Fa[:, :, FB_NVC * LANE:(FB_NVC + 1) * LANE]]).reshape(2, B, ncp, NSA_CMP_STRIDE * LANE)
        ptop, pbot, w1e, w2e = _compress_weights(nsa_cmp_pos[l], nsa_cmp_w1[l], nsa_cmp_w2[l])
        kv2 = _compress(r2, ptop, pbot, w1e, w2e)
        oc, sel = _cmp_attn(Ga, kv2, ovl)
        osel = _flash(Ga, Ga, GvT, npairs=4, t=t_att, q_blk=(GB_NQ, 1), k_blk=(GB_NKS, 0),
                      v_blk=(GB_NVS - GB_DV, 0), bias=bias_nsa, sel=sel, emat=emat, out_dtype=F32)
        ow = _flash(Ga, Ga, GvT, npairs=4, t=t_att, window=NSA_WINDOW, q_blk=(GB_NQ, 1), k_blk=(GB_NKW, 0),
                    v_blk=(GB_NVW - GB_DV, 0), bias=bias_nsa, out_dtype=F32)

        lam_init = 0.8 - 0.6 * math.exp(-0.3 * l)
        lp = diff_lambda[l].astype(F32)
        lam = jnp.exp(jnp.sum(lp[0] * lp[1])) - jnp.exp(jnp.sum(lp[2] * lp[3])) + lam_init
        lam_arr = jnp.concatenate([jnp.full((1, LANE), 1.0, F32) * lam,
                                   jnp.full((7, LANE), 1.0 - lam_init, F32)], 0)
        yd = _flash(Ga, Ga, GvT, npairs=DIFF_HEADS, t=t_att, q_blk=(GB_DQ, 1), k_blk=(GB_DK, 1), v_blk=(0, 1),
                    bias=bias_diff, bias_shared=True, diff=(lam_arr, diff_subln[l][None]))

        wbr = jnp.stack([w_branch[l, 0], _pair_rows(w_branch[l, 1]), _pair_rows(w_branch[l, 2]),
                         w_branch[l, 3]]).astype(BF16)
        f2 = lambda a: a.reshape(T, a.shape[-1])
        h = _merge(h, gl[1], f2(ya), f2(yb), f2(oc), f2(osel), f2(ow), f2(Fa), f2(yd),
                   w_gate[l].astype(BF16), wbr, w_o[l].astype(BF16), p3)

        h = _ffn(h, gl[2], ffn_w_gate[l, 1].astype(BF16), ffn_w_up[l, 1].astype(BF16),
                 ffn_w_down[l, 1].astype(BF16), final_g[None], final=(l == depth - 1))
    return h.reshape(B, S, D)
```

```python
import functools
import math

import numpy as np
import jax
import jax.numpy as jnp
from jax import lax
from jax.experimental import pallas as pl
from jax.experimental.pallas import tpu as pltpu

F32 = jnp.float32
BF16 = jnp.bfloat16

D_MODEL = 1024
EPS = 1e-6
NEG = -1e30
D_FF = 2816

MLA_HEADS = 8
MLA_Q_RANK = 256
MLA_KV_RANK = 128
MLA_NOPE = 64
MLA_ROPE = 32
MLA_V = 64
ROPE_THETA = 10000.0

SWA_HEADS = 8
SWA_WINDOW = 128

NSA_HEADS = 8
NSA_HD = 64
NSA_CMP_LEN = 32
NSA_CMP_STRIDE = 16
NSA_CMP_HIDDEN = 128
NSA_SEL_LEN = 64
NSA_TOPK = 16
NSA_WINDOW = 512
NSA_FORCE = 1e4

DIFF_HEADS = 4
DIFF_HD = 64

N_BUCKETS = 32
MAX_DIST = 128

LANE = 128
HALF = 64
VMEM_LIMIT = 56 * 1024 * 1024
LOG2E = math.log2(math.e)
KEY_CHUNK = 256

O_CQ, O_CKV, O_KROPE, O_SQ, O_SK, O_SV, O_NQ = 0, 256, 384, 416, 928, 1056, 1184
O_NKC, O_NVC, O_NKS, O_NVS, O_NKW, O_NVW, O_NGATE = 1696, 1824, 1952, 2080, 2208, 2336, 2464
O_DQ, O_DK, O_DV = 2488, 3000, 3512

F_COLS = 1024
FB_CKV, FB_KROPE, FB_KSWAP, FB_NKC, FB_NVC, FB_NGATE = 2, 3, 4, 5, 6, 7
G_COLS = 3328
GB_SQ, GB_NQ, GB_DQ, GB_DK, GB_DV = 0, 4, 8, 12, 16
GB_SK, GB_SV, GB_NKS, GB_NVS, GB_NKW, GB_NVW = 20, 21, 22, 23, 24, 25


def _cparams(sem):
    return pltpu.CompilerParams(dimension_semantics=sem, vmem_limit_bytes=VMEM_LIMIT)


def _rms(x, g):
    return x * lax.rsqrt(jnp.mean(x * x, -1, keepdims=True) + EPS) * g


def _dot(a, b):
    return jnp.dot(a, b, preferred_element_type=F32)


def _dot_nt(a, b):
    return lax.dot_general(a, b, (((1,), (1,)), ((), ())), preferred_element_type=F32)


def _split3(x):
    hi = x.astype(BF16)
    r = x - hi.astype(F32)
    mid = r.astype(BF16)
    lo = (r - mid.astype(F32)).astype(BF16)
    return hi, mid, lo


def _dot_exact(x, w01):
    hi, mid, lo = _split3(x)
    return _dot(hi, w01) + _dot(mid, w01) + _dot(lo, w01)


def _ffn_kernel(h_ref, g_ref, wg_ref, wu_ref, wd_ref, gf_ref, o_ref, n_scr, acc_scr, *, final):
    j = pl.program_id(1)

    @pl.when(j == 0)
    def _():
        n_scr[...] = _rms(h_ref[...], g_ref[...]).astype(BF16)
        acc_scr[...] = jnp.zeros_like(acc_scr)

    n = n_scr[...]
    a = _dot(n, wg_ref[...])
    b = _dot(n, wu_ref[...])
    t = (a * jax.nn.sigmoid(a) * b).astype(BF16)
    acc_scr[...] += _dot(t, wd_ref[...])

    @pl.when(j == pl.num_programs(1) - 1)
    def _():
        out = h_ref[...] + 0.5 * acc_scr[...]
        if final:
            out = _rms(out, gf_ref[...])
        o_ref[...] = out


def _ffn(h, g, wg, wu, wd, gf, final):
    T = h.shape[0]
    tm, tf = 512, D_FF // 2
    return pl.pallas_call(
        functools.partial(_ffn_kernel, final=final),
        grid=(T // tm, D_FF // tf),
        in_specs=[
            pl.BlockSpec((tm, D_MODEL), lambda i, j: (i, 0)),
            pl.BlockSpec((1, D_MODEL), lambda i, j: (0, 0)),
            pl.BlockSpec((D_MODEL, tf), lambda i, j: (0, j)),
            pl.BlockSpec((D_MODEL, tf), lambda i, j: (0, j)),
            pl.BlockSpec((tf, D_MODEL), lambda i, j: (j, 0)),
            pl.BlockSpec((1, D_MODEL), lambda i, j: (0, 0)),
        ],
        out_specs=pl.BlockSpec((tm, D_MODEL), lambda i, j: (i, 0)),
        out_shape=jax.ShapeDtypeStruct((T, D_MODEL), F32),
        scratch_shapes=[pltpu.VMEM((tm, D_MODEL), BF16), pltpu.VMEM((tm, D_MODEL), F32)],
        compiler_params=_cparams(("parallel", "arbitrary")),
        name="ffn_half_step",
    )(h, g, wg, wu, wd, gf)


def _proj_kernel(h_ref, g_ref, w_ref, f_ref, gq_ref):
    u = _rms(h_ref[...], g_ref[...]).astype(BF16)
    f_ref[...] = _dot(u, w_ref[:, :F_COLS])
    step = 512
    for c0 in range(0, G_COLS, step):
        c1 = min(c0 + step, G_COLS)
        gq_ref[:, c0:c1] = _dot(u, w_ref[:, F_COLS + c0:F_COLS + c1]).astype(BF16)


def _proj(h, g, wp):
    T = h.shape[0]
    tm = 512
    return pl.pallas_call(
        _proj_kernel,
        grid=(T // tm,),
        in_specs=[
            pl.BlockSpec((tm, D_MODEL), lambda i: (i, 0)),
            pl.BlockSpec((1, D_MODEL), lambda i: (0, 0)),
            pl.BlockSpec((D_MODEL, F_COLS + G_COLS), lambda i: (0, 0)),
        ],
        out_specs=[
            pl.BlockSpec((tm, F_COLS), lambda i: (i, 0)),
            pl.BlockSpec((tm, G_COLS), lambda i: (i, 0)),
        ],
        out_shape=[jax.ShapeDtypeStruct((T, F_COLS), F32),
                   jax.ShapeDtypeStruct((T, G_COLS), BF16)],
        compiler_params=_cparams(("parallel",)),
        name="in_proj",
    )(h, g, wp)


def _proj_weight(w_in):
    ncol = F_COLS + G_COLS
    src = np.zeros((ncol,), np.int32)
    scale = np.zeros((ncol,), np.float32)

    def put(dst, srcs, s=1.0):
        srcs = np.asarray(srcs)
        src[dst:dst + len(srcs)] = srcs
        scale[dst:dst + len(srcs)] = s

    half = MLA_ROPE // 2
    put(0, O_CQ + np.arange(MLA_Q_RANK))
    put(FB_CKV * LANE, O_CKV + np.arange(MLA_KV_RANK))
    put(FB_KROPE * LANE + MLA_NOPE, O_KROPE + np.arange(MLA_ROPE))
    put(FB_KSWAP * LANE + MLA_NOPE, O_KROPE + np.concatenate([np.arange(half, MLA_ROPE), np.arange(half)]))
    put(FB_NKC * LANE, O_NKC + np.arange(LANE))
    put(FB_NVC * LANE, O_NVC + np.arange(LANE))
    put(FB_NGATE * LANE, O_NGATE + np.arange(3 * NSA_HEADS))
    g0 = F_COLS
    qscale = 0.125 * LOG2E
    pair_perm = np.concatenate([np.concatenate([(j) * HALF + np.arange(HALF), (4 + j) * HALF + np.arange(HALF)])
                                for j in range(4)])
    put(g0 + GB_SQ * LANE, O_SQ + pair_perm, qscale)
    put(g0 + GB_NQ * LANE, O_NQ + pair_perm, qscale)
    put(g0 + GB_DQ * LANE, O_DQ + np.arange(512), qscale)
    put(g0 + GB_DK * LANE, O_DK + np.arange(512))
    put(g0 + GB_DV * LANE, O_DV + np.arange(512))
    for blk, off in ((GB_SK, O_SK), (GB_SV, O_SV), (GB_NKS, O_NKS), (GB_NVS, O_NVS),
                     (GB_NKW, O_NKW), (GB_NVW, O_NVW)):
        put(g0 + blk * LANE, off + np.arange(LANE))
    return (w_in[:, src] * scale[None, :]).astype(BF16)


PAIR_HEAD = np.array([[j, 4 + j] for j in range(4)])


def _mla_prep_kernel(cq_ref, ckv_ref, kr_ref, ks_ref, ct_ref, st_ref, qg_ref, kg_ref,
                     wa_ref, wb_ref, wk_ref, wv_ref, q_ref, k_ref, v_ref, *, scale):
    qn = _rms(cq_ref[0], qg_ref[...]).astype(BF16)
    kn = _rms(ckv_ref[0], kg_ref[...]).astype(BF16)
    ct = ct_ref[...]
    st = st_ref[...]
    krot = kr_ref[0] * ct + ks_ref[0] * st
    v_ref[0] = _dot(kn, wv_ref[...]).astype(BF16)
    for h in range(MLA_HEADS):
        sl = slice(h * LANE, (h + 1) * LANE)
        qa = _dot(qn, wa_ref[:, sl])
        qb = _dot(qn, wb_ref[:, sl])
        q_ref[0, :, sl] = ((qa * ct + qb * st) * scale).astype(BF16)
        k_ref[0, :, sl] = (_dot(kn, wk_ref[:, sl]) + krot).astype(BF16)


def _mla_prep(Fa, ctab, stab, qg, kg, wa, wb, wk, wv):
    B, S, _ = Fa.shape
    tm = 512
    HW = MLA_HEADS * LANE
    full = lambda shape: pl.BlockSpec(shape, lambda b, i: (0,) * len(shape))
    return pl.pallas_call(
        functools.partial(_mla_prep_kernel, scale=(MLA_NOPE + MLA_ROPE) ** -0.5 * LOG2E),
        grid=(B, S // tm),
        in_specs=[
            pl.BlockSpec((1, tm, MLA_Q_RANK), lambda b, i: (b, i, 0)),
            pl.BlockSpec((1, tm, LANE), lambda b, i: (b, i, FB_CKV)),
            pl.BlockSpec((1, tm, LANE), lambda b, i: (b, i, FB_KROPE)),
            pl.BlockSpec((1, tm, LANE), lambda b, i: (b, i, FB_KSWAP)),
            pl.BlockSpec((tm, LANE), lambda b, i: (i, 0)),
            pl.BlockSpec((tm, LANE), lambda b, i: (i, 0)),
            full((1, MLA_Q_RANK)), full((1, MLA_KV_RANK)),
            full((MLA_Q_RANK, HW)), full((MLA_Q_RANK, HW)),
            full((MLA_KV_RANK, HW)), full((MLA_KV_RANK, MLA_HEADS * MLA_V)),
        ],
        out_specs=[
            pl.BlockSpec((1, tm, HW), lambda b, i: (b, i, 0)),
            pl.BlockSpec((1, tm, HW), lambda b, i: (b, i, 0)),
            pl.BlockSpec((1, tm, MLA_HEADS * MLA_V), lambda b, i: (b, i, 0)),
        ],
        out_shape=[jax.ShapeDtypeStruct((B, S, HW), BF16),
                   jax.ShapeDtypeStruct((B, S, HW), BF16),
                   jax.ShapeDtypeStruct((B, S, MLA_HEADS * MLA_V), BF16)],
        compiler_params=_cparams(("parallel", "parallel")),
        name="mla_prep",
    )(Fa, Fa, Fa, Fa, ctab, stab, qg, kg, wa, wb, wk, wv)


def _mla_weights(w_uq, w_ukv):
    dq = MLA_NOPE + MLA_ROPE
    half = MLA_ROPE // 2
    z = lambda n: jnp.zeros((w_uq.shape[0], n), F32)
    wa, wb, wk, wv = [], [], [], []
    for h in range(MLA_HEADS):
        nope = w_uq[:, h * dq:h * dq + MLA_NOPE]
        r = w_uq[:, h * dq + MLA_NOPE:(h + 1) * dq]
        wa += [nope, r, z(LANE - dq)]
        wb += [z(MLA_NOPE), r[:, half:], r[:, :half], z(LANE - dq)]
        wk += [w_ukv[:, h * LANE:h * LANE + MLA_NOPE], jnp.zeros((w_ukv.shape[0], LANE - MLA_NOPE), F32)]
        wv += [w_ukv[:, h * LANE + MLA_NOPE:(h + 1) * LANE]]
    cat = lambda xs: jnp.concatenate(xs, axis=1).astype(BF16)
    return cat(wa), cat(wb), cat(wk), cat(wv)


def _rope_tables(S):
    half = MLA_ROPE // 2
    freqs = ROPE_THETA ** (-jnp.arange(half, dtype=F32) / half)
    ang = jnp.arange(S, dtype=jnp.int32)[:, None].astype(F32) * freqs
    cos, sin = jnp.cos(ang), jnp.sin(ang)
    ones = jnp.ones((S, MLA_NOPE), F32)
    zeros = jnp.zeros((S, LANE - MLA_NOPE - MLA_ROPE), F32)
    ctab = jnp.concatenate([ones, cos, cos, zeros], 1)
    stab = jnp.concatenate([jnp.zeros((S, MLA_NOPE), F32), -sin, sin, zeros], 1)
    return ctab, stab


def _t5_bucket(dist):
    n = jnp.maximum(dist, 0)
    max_exact = N_BUCKETS // 2
    nf = jnp.maximum(n, 1).astype(F32)
    large = max_exact + (jnp.log(nf / max_exact) / math.log(MAX_DIST / max_exact)
                         * (N_BUCKETS - max_exact)).astype(jnp.int32)
    large = jnp.minimum(large, N_BUCKETS - 1)
    return jnp.where(n < max_exact, n, large)


def _bias_tiles(tab, t):
    assert t >= MAX_DIST
    i = jnp.arange(t, dtype=jnp.int32)[:, None]
    j = jnp.arange(t, dtype=jnp.int32)[None, :]
    tiles = []
    tab_t = tab.T[:, :, None, None]
    for rel in range(2):
        b = _t5_bucket(rel * t + j - i)[None]
        tile = jnp.zeros((tab.shape[1], t, t), F32)
        for n in range(N_BUCKETS):
            tile = jnp.where(b == n, tab_t[:, n], tile)
        tiles.append(tile)
    far = tab[N_BUCKETS - 1][:, None, None, None]
    return (jnp.stack(tiles, 1) - far) * LOG2E


def _flash_kernel(*refs, t, kc, window, shared_qk, has_bias, bias_shared, has_sel, has_sink, diff, out_dtype):
    qt_ref, kt_ref, first_ref, last_ref, rel_ref = refs[:5]
    refs = list(refs[5:])
    if shared_qk:
        q_ref, k_ref = refs[:2]
        refs = refs[2:]
    else:
        qa_ref, qb_ref, ka_ref, kb_ref = refs[:4]
        refs = refs[4:]
    vt_ref = refs.pop(0)
    if has_bias:
        ba_ref = refs.pop(0)
        bb_ref = ba_ref if bias_shared else refs.pop(0)
    if has_sel:
        sa_ref, sb_ref, et_ref = refs[:3]
        refs = refs[3:]
    if has_sink:
        sink_ref = refs.pop(0)
    if diff:
        lam_ref, sub_ref = refs[:2]
        refs = refs[2:]
    o_ref, m_scr, l_scr, acc_scr, p_scr = refs

    step = pl.program_id(2)
    qt = qt_ref[step]
    kt = kt_ref[step]
    lo = lax.broadcasted_iota(jnp.int32, (1, LANE), 1) < HALF

    @pl.when(first_ref[step] == 1)
    def _():
        for a in range(2):
            if has_sink:
                m_scr[a] = jnp.broadcast_to(sink_ref[0, a:a + 1, 0:1], (1, t))
                l_scr[a] = jnp.ones((1, t), F32)
            else:
                m_scr[a] = jnp.full((1, t), NEG, F32)
                l_scr[a] = jnp.zeros((1, t), F32)
            acc_scr[a] = jnp.zeros((LANE, t), F32)

    if shared_qk:
        q = q_ref[0]
        zero = jnp.zeros_like(q)
        qs = (jnp.where(lo, q, zero), jnp.where(lo, zero, q))
        ks = (k_ref[0], k_ref[0])
    else:
        qs = (qa_ref[0], qb_ref[0])
        ks = (ka_ref[0], kb_ref[0])
    vt = vt_ref[0]

    def process(rel):
        col_slices = [slice(c, c + LANE) for c in range(0, t, LANE)]

        def unit_range(kh, c):
            base = rel * t + c - kh
            return base, base - (kc - 1), base + (LANE - 1)

        def unit_dead(kh, c):
            _, dmin, dmax = unit_range(kh, c)
            return (rel == 0 and dmax < 0) or (window is not None and dmin >= window)

        jobs = [(kh, a) for kh in range(0, t, kc) for a in range(2)
                if not all(unit_dead(kh, cs.start) for cs in col_slices)]

        def scores(job):
            kh, a = job
            kk, qq = ks[a][kh:kh + kc], qs[a]
            if has_sel:
                kk = jnp.concatenate([kk, et_ref[kh:kh + kc, :]], axis=1)
                qq = jnp.concatenate([qq, (sa_ref if a == 0 else sb_ref)[0]], axis=1)
            return _dot_nt(kk, qq)

        m = [[m_scr[a, :, cs] for cs in col_slices] for a in range(2)]
        l = [[l_scr[a, :, cs] for cs in col_slices] for a in range(2)]
        acc = [acc_scr[0], acc_scr[1]]
        nxt = scores(jobs[0])
        for ji, (kh, a) in enumerate(jobs):
            keys = slice(kh, kh + kc)
            st = nxt
            if ji + 1 < len(jobs):
                nxt = scores(jobs[ji + 1])
            b_ref = (ba_ref if a == 0 else bb_ref) if has_bias else None
            alphas = []
            for ci, cols in enumerate(col_slices):
                base, dmin, dmax = unit_range(kh, cols.start)
                if unit_dead(kh, cols.start):
                    alphas.append(jnp.ones((1, LANE), F32))
                    p_scr[a, keys, cols] = jnp.zeros((kc, LANE), BF16)
                    continue
                s = st[:, cols]
                if has_bias and rel < 2 and dmin < MAX_DIST:
                    s = s + b_ref[0, 0, 0, keys, cols]
                need_lo = rel == 0 and dmin < 0
                need_hi = window is not None and dmax >= window
                if need_lo or need_hi:
                    d = (lax.broadcasted_iota(jnp.int32, (kc, LANE), 1)
                         - lax.broadcasted_iota(jnp.int32, (kc, LANE), 0)) + base
                    valid = (d >= 0) if need_lo else None
                    if need_hi:
                        valid = (d < window) if valid is None else (valid & (d < window))
                    s = jnp.where(valid, s, NEG)
                m_prev = m[a][ci]
                m_new = jnp.maximum(m_prev, jnp.max(s, 0, keepdims=True))
                alpha = jnp.exp2(m_prev - m_new)
                p = jnp.exp2(s - m_new)
                l[a][ci] = alpha * l[a][ci] + jnp.sum(p, 0, keepdims=True)
                m[a][ci] = m_new
                alphas.append(alpha)
                p_scr[a, keys, cols] = p.astype(BF16)
            acc[a] = jnp.concatenate(alphas, axis=1) * acc[a] + _dot(vt[:, keys], p_scr[a, keys, :])
        for a in range(2):
            for ci, cs in enumerate(col_slices):
                m_scr[a, :, cs] = m[a][ci]
                l_scr[a, :, cs] = l[a][ci]
            acc_scr[a] = acc[a]

    for r in range(2 if window is not None else 3):
        @pl.when(rel_ref[step] == r)
        def _(r=r):
            process(r)

    @pl.when(last_ref[step] == 1)
    def _():
        o0 = acc_scr[0] / l_scr[0]
        o1 = acc_scr[1] / l_scr[1]
        if diff:
            o = (o0 - lam_ref[0:1, 0:1] * o1).T
            o = _rms(o, sub_ref[...]) * lam_ref[1:2, 0:1]
        else:
            feat = lax.broadcasted_iota(jnp.int32, (LANE, 1), 0)
            o = jnp.where(feat < HALF, o0, o1).T
        o_ref[0] = o.astype(out_dtype)


def _flash(q_arr, k_arr, vt_arr, *, npairs, t, q_blk, k_blk, v_blk, window=None, shared_qk=True,
           bias=None, bias_shared=False, sel=None, emat=None, sinks=None, diff=None, out_dtype=BF16):
    B, S, _ = q_arr.shape
    nq = S // t
    nprev = None if window is None else -(-(window - 1) // t)
    assert window is None or window <= t + 1, "a window must end in the previous tile"
    steps = []
    for qi in range(nq):
        k0 = 0 if nprev is None else max(0, qi - nprev)
        for ki in range(k0, qi + 1):
            steps.append((qi, ki, int(ki == k0), int(ki == qi), min(qi - ki, 2)))
    tabs = [jnp.asarray(np.array([s[c] for s in steps], np.int32)) for c in range(5)]
    nsteps = len(steps)
    kc = min(t, KEY_CHUNK)

    def qmap(off, stride):
        return lambda b, p, s, qt, kt, fr, la, rl: (b, qt[s], off + stride * p)

    def kmap(off, stride):
        return lambda b, p, s, qt, kt, fr, la, rl: (b, kt[s], off + stride * p)

    in_specs, args = [], []
    if shared_qk:
        in_specs += [pl.BlockSpec((1, t, LANE), qmap(*q_blk)), pl.BlockSpec((1, t, LANE), kmap(*k_blk))]
        args += [q_arr, k_arr]
    else:
        in_specs += [pl.BlockSpec((1, t, LANE), qmap(*q_blk[0])), pl.BlockSpec((1, t, LANE), qmap(*q_blk[1])),
                     pl.BlockSpec((1, t, LANE), kmap(*k_blk[0])), pl.BlockSpec((1, t, LANE), kmap(*k_blk[1]))]
        args += [q_arr, q_arr, k_arr, k_arr]
    voff, vstride = v_blk
    in_specs.append(pl.BlockSpec((1, LANE, t),
                                 lambda b, p, s, qt, kt, fr, la, rl: (b, voff + vstride * p, kt[s])))
    args.append(vt_arr)
    if bias is not None:
        for a in range(1 if bias_shared else 2):
            in_specs.append(pl.BlockSpec((1, 1, 1, t, t),
                                         lambda b, p, s, qt, kt, fr, la, rl, a=a: (a, p, jnp.minimum(rl[s], 1), 0, 0)))
            args.append(bias)
    if sel is not None:
        nselp = emat.shape[1]
        for a in range(2):
            in_specs.append(pl.BlockSpec((1, t, nselp),
                                         lambda b, p, s, qt, kt, fr, la, rl, a=a: (b, qt[s], a)))
            args.append(sel)
        in_specs.append(pl.BlockSpec((t, nselp), lambda b, p, s, qt, kt, fr, la, rl: (kt[s], 0)))
        args.append(emat)
    if sinks is not None:
        in_specs.append(pl.BlockSpec((1, 8, LANE), lambda b, p, s, qt, kt, fr, la, rl: (p, 0, 0)))
        args.append(sinks)
    if diff is not None:
        lam_arr, sub = diff
        in_specs.append(pl.BlockSpec((8, LANE), lambda b, p, s, qt, kt, fr, la, rl: (0, 0)))
        in_specs.append(pl.BlockSpec((1, LANE), lambda b, p, s, qt, kt, fr, la, rl: (0, 0)))
        args += [lam_arr, sub]

    kern = functools.partial(
        _flash_kernel, t=t, kc=kc, window=window, shared_qk=shared_qk, has_bias=bias is not None,
        bias_shared=bias_shared, has_sel=sel is not None, has_sink=sinks is not None,
        diff=diff is not None, out_dtype=out_dtype)
    return pl.pallas_call(
        kern,
        grid_spec=pltpu.PrefetchScalarGridSpec(
            num_scalar_prefetch=5,
            grid=(B, npairs, nsteps),
            in_specs=in_specs,
            out_specs=pl.BlockSpec((1, t, LANE), lambda b, p, s, qt, kt, fr, la, rl: (b, qt[s], p)),
            scratch_shapes=[pltpu.VMEM((2, 1, t), F32), pltpu.VMEM((2, 1, t), F32),
                            pltpu.VMEM((2, LANE, t), F32), pltpu.VMEM((2, t, t), BF16)],
        ),
        out_shape=jax.ShapeDtypeStruct((B, S, npairs * LANE), out_dtype),
        compiler_params=_cparams(("parallel", "parallel", "arbitrary")),
        name="flash_pairs",
    )(*tabs, *args)


def _compress_kernel(r_ref, pt_ref, pb_ref, w1_ref, w2_ref, o_ref):
    r = r_ref[0, 0]
    n = r.shape[0]
    xt = (r + pt_ref[0]).astype(BF16)
    xb = (r + pb_ref[0]).astype(BF16)
    out = jnp.zeros((n, LANE), F32)
    for h in range(2):
        top = _dot(xt, w1_ref[0, h, 0])
        bot = _dot(xb, w1_ref[0, h, 1])
        x = top + pltpu.roll(bot, n - 1, 0)
        hd = 0.5 * x * (1.0 + jnp.tanh(math.sqrt(2.0 / math.pi) * (x + 0.044715 * (x * x * x))))
        out = out + _dot(hd.astype(BF16), w2_ref[0, h])
    o_ref[0, 0] = out.astype(BF16)


def _compress(r2, ptop, pbot, w1e, w2e):
    _, B, n, K = r2.shape
    return pl.pallas_call(
        _compress_kernel,
        grid=(2, B),
        in_specs=[
            pl.BlockSpec((1, 1, n, K), lambda c, b: (c, b, 0, 0)),
            pl.BlockSpec((1, 1, K), lambda c, b: (c, 0, 0)),
            pl.BlockSpec((1, 1, K), lambda c, b: (c, 0, 0)),
            pl.BlockSpec((1, 2, 2, K, NSA_CMP_HIDDEN), lambda c, b: (c, 0, 0, 0, 0)),
            pl.BlockSpec((1, 2, NSA_CMP_HIDDEN, LANE), lambda c, b: (c, 0, 0, 0)),
        ],
        out_specs=pl.BlockSpec((1, 1, n, LANE), lambda c, b: (c, b, 0, 0)),
        out_shape=jax.ShapeDtypeStruct((2, B, n, LANE), BF16),
        compiler_params=_cparams(("parallel", "parallel")),
        name="nsa_compress",
    )(r2, ptop, pbot, w1e, w2e)


def _compress_weights(cmp_pos, cmp_w1, cmp_w2):
    L2 = NSA_CMP_STRIDE
    d = NSA_HD
    w1 = cmp_w1.reshape(2, 2, L2, d, NSA_CMP_HIDDEN)
    z = jnp.zeros_like(w1)
    e0 = jnp.concatenate([w1, z], axis=3)
    e1 = jnp.concatenate([z, w1], axis=3)
    w1e = jnp.stack([e0, e1], axis=1).reshape(2, 2, 2, L2 * 2 * d, NSA_CMP_HIDDEN).astype(BF16)
    pos = cmp_pos.reshape(2, 2, L2, 1, d)
    pos = jnp.broadcast_to(pos, (2, 2, L2, 2, d)).reshape(2, 2, 1, L2 * 2 * d)
    zw = jnp.zeros_like(cmp_w2)
    w2e = jnp.stack([jnp.concatenate([cmp_w2, zw], -1), jnp.concatenate([zw, cmp_w2], -1)], 1).astype(BF16)
    return pos[:, 0], pos[:, 1], w1e, w2e


def _cmp_attn_kernel(q_ref, kc_ref, vc_ref, ov_ref, oc_ref, sel_ref, *, tq, ncp, nselp):
    i = pl.program_id(1)
    qpos = i * tq + lax.broadcasted_iota(jnp.int32, (tq, 1), 0)
    cidx = lax.broadcasted_iota(jnp.int32, (1, ncp), 1)
    cvalid = (cidx * NSA_CMP_STRIDE + (NSA_CMP_LEN - 1)) <= qpos
    lane = lax.broadcasted_iota(jnp.int32, (1, LANE), 1)
    lo = lane < HALF
    kc = kc_ref[0, 0]
    vc = vc_ref[0, 0]
    psum = [jnp.zeros((tq, ncp), F32), jnp.zeros((tq, ncp), F32)]
    for j in range(4):
        q2 = q_ref[0, :, j * LANE:(j + 1) * LANE]
        zero = jnp.zeros_like(q2)
        outs = []
        for a in range(2):
            qa = jnp.where(lo, q2, zero) if a == 0 else jnp.where(lo, zero, q2)
            s = jnp.where(cvalid, _dot_nt(qa, kc), NEG)
            m = jnp.max(s, -1, keepdims=True)
            e = jnp.where(cvalid, jnp.exp2(s - m), 0.0)
            den = jnp.sum(e, -1, keepdims=True)
            p = e / jnp.where(den > 0, den, 1.0)
            psum[a] = psum[a] + p
            outs.append(_dot(p.astype(BF16), vc))
        oc_ref[0, :, j * LANE:(j + 1) * LANE] = jnp.where(lo, outs[0], outs[1])

    n = lax.broadcasted_iota(jnp.int32, (1, nselp), 1)
    cur = qpos >> int(math.log2(NSA_SEL_LEN))
    causal = (n * NSA_SEL_LEN) <= qpos
    forced = (n == 0) | (n == cur) | (n == cur - 1)
    nf = n.astype(F32)
    for a in range(2):
        imp = _dot_exact(psum[a], ov_ref[...])
        x = jnp.where(causal, jnp.where(forced, NSA_FORCE, imp), -1.0)
        picked = jnp.zeros((tq, nselp), F32)
        for _ in range(NSA_TOPK):
            mx = jnp.max(x, -1, keepdims=True)
            idx = jnp.min(jnp.where(x == mx, nf, float(nselp)), -1, keepdims=True)
            hit = nf == idx
            picked = jnp.where(hit, 1.0, picked)
            x = jnp.where(hit, -2.0, x)
        sel_ref[0, :, a * nselp:(a + 1) * nselp] = jnp.where(causal & (picked > 0.5), 0.0, NEG).astype(BF16)


def _cmp_attn(G3, kv2, ovl):
    B, S, _ = G3.shape
    ncp = kv2.shape[2]
    nselp = ovl.shape[1]
    tq = 256
    return pl.pallas_call(
        functools.partial(_cmp_attn_kernel, tq=tq, ncp=ncp, nselp=nselp),
        grid=(B, S // tq),
        in_specs=[
            pl.BlockSpec((1, tq, 4 * LANE), lambda b, i: (b, i, GB_NQ // 4)),
            pl.BlockSpec((1, 1, ncp, LANE), lambda b, i: (0, b, 0, 0)),
            pl.BlockSpec((1, 1, ncp, LANE), lambda b, i: (1, b, 0, 0)),
            pl.BlockSpec((ncp, nselp), lambda b, i: (0, 0)),
        ],
        out_specs=[
            pl.BlockSpec((1, tq, 4 * LANE), lambda b, i: (b, i, 0)),
            pl.BlockSpec((1, tq, 2 * nselp), lambda b, i: (b, i, 0)),
        ],
        out_shape=[jax.ShapeDtypeStruct((B, S, 4 * LANE), F32),
                   jax.ShapeDtypeStruct((B, S, 2 * nselp), BF16)],
        compiler_params=_cparams(("parallel", "parallel")),
        name="nsa_cmp_attn_topk",
    )(G3, kv2, kv2, ovl)


def _merge_kernel(h_ref, g_ref, ya_ref, yb_ref, oc_ref, os_ref, ow_ref, ng_ref, yd_ref,
                  wg_ref, wb_ref, wo_ref, p3_ref, o_ref):
    h = h_ref[...]
    u = _rms(h, g_ref[...]).astype(BF16)
    gexp = _dot_exact(jax.nn.sigmoid(ng_ref[...]), p3_ref[...])
    W = 4 * LANE
    yc = gexp[:, :W] * oc_ref[...] + gexp[:, W:2 * W] * os_ref[...] + gexp[:, 2 * W:] * ow_ref[...]
    ys = (ya_ref[...], yb_ref[...], yc.astype(BF16), yd_ref[...])
    merged = jnp.zeros(h.shape, F32)
    for i in range(4):
        gate = jax.nn.sigmoid(_dot(u, wg_ref[i]))
        merged = merged + gate * _dot(ys[i], wb_ref[i])
    o_ref[...] = h + _dot(merged.astype(BF16), wo_ref[...])


def _merge(h, g, ya, yb, oc, osel, ow, Fa, yd, wg, wb, wo, p3):
    T = h.shape[0]
    tm = 256
    W = 4 * LANE
    row = lambda w: pl.BlockSpec((tm, w), lambda i: (i, 0))
    const = lambda shape: pl.BlockSpec(shape, lambda i: (0,) * len(shape))
    return pl.pallas_call(
        _merge_kernel,
        grid=(T // tm,),
        in_specs=[
            row(D_MODEL), const((1, D_MODEL)), row(W), row(W), row(W), row(W), row(W),
            pl.BlockSpec((tm, LANE), lambda i: (i, FB_NGATE)), row(W),
            const((4, D_MODEL, D_MODEL)), const((4, W, D_MODEL)), const((D_MODEL, D_MODEL)),
            const((LANE, 3 * W)),
        ],
        out_specs=row(D_MODEL),
        out_shape=jax.ShapeDtypeStruct((T, D_MODEL), F32),
        compiler_params=_cparams(("parallel",)),
        name="gated_merge",
    )(h, g, ya, yb, oc, osel, ow, Fa, yd, wg, wb, wo, p3)


def _gate_expand_matrix():
    W = 4 * LANE
    m = np.zeros((LANE, 3 * W), np.float32)
    for j in range(4):
        for a in range(2):
            head = PAIR_HEAD[j, a]
            for r in range(3):
                c0 = r * W + j * LANE + a * HALF
                m[head * 3 + r, c0:c0 + HALF] = 1.0
    return jnp.asarray(m, BF16)


def _pair_rows(w):
    perm = np.concatenate([np.arange(HALF) + PAIR_HEAD[j, a] * HALF for j in range(4) for a in range(2)])
    return w[perm]


def kernel(x, norm_g, w_in, mla_q_norm, mla_kv_norm, mla_w_uq, mla_w_ukv, swa_sinks,
           nsa_cmp_pos, nsa_cmp_w1, nsa_cmp_w2, diff_lambda, diff_subln, rel_bias_table,
           w_branch, w_gate, w_o, ffn_w_gate, ffn_w_up, ffn_w_down, final_g):
    B, S, D = x.shape
    T = B * S
    depth = w_in.shape[0]
    t_att = 512
    t_big = min(S, 1024)
    ncp = S // NSA_CMP_STRIDE
    nsel = S // NSA_SEL_LEN
    nselp = -(-nsel // LANE) * LANE

    tab_swa = rel_bias_table[:, :SWA_HEADS]
    tab_nsa = rel_bias_table[:, SWA_HEADS:SWA_HEADS + NSA_HEADS]
    tab_diff = rel_bias_table[:, SWA_HEADS + NSA_HEADS:]
    pair_heads = PAIR_HEAD.T.reshape(-1)
    bias_swa = _bias_tiles(tab_swa[:, pair_heads], t_att).reshape(2, 4, 2, t_att, t_att)
    bias_nsa = _bias_tiles(tab_nsa[:, pair_heads], t_att).reshape(2, 4, 2, t_att, t_att)
    bias_sel = _bias_tiles(tab_nsa[:, pair_heads], t_big).reshape(2, 4, 2, t_big, t_big)
    bias_diff = _bias_tiles(tab_diff, t_big)[None]
    ctab, stab = _rope_tables(S)

    ci = np.arange(ncp)[:, None]
    ni = np.arange(nselp)[None, :]
    nc_real = (S - NSA_CMP_LEN) // NSA_CMP_STRIDE + 1
    ovl = ((ci * NSA_CMP_STRIDE < ni * NSA_SEL_LEN + NSA_SEL_LEN)
           & (ni * NSA_SEL_LEN < ci * NSA_CMP_STRIDE + NSA_CMP_LEN) & (ci < nc_real) & (ni < nsel))
    ovl = jnp.asarray(ovl.astype(np.float32), BF16)
    emat = jnp.asarray((np.arange(S)[:, None] // NSA_SEL_LEN == np.arange(nselp)[None, :]).astype(np.float32), BF16)
    p3 = _gate_expand_matrix()

    h = x.reshape(T, D)
    for l in range(depth):
        gl = norm_g[l].reshape(3, 1, D)
        h = _ffn(h, gl[0], ffn_w_gate[l, 0].astype(BF16), ffn_w_up[l, 0].astype(BF16),
                 ffn_w_down[l, 0].astype(BF16), gl[0], final=False)

        Fa, Ga = _proj(h, gl[1], _proj_weight(w_in[l]))
        Fa = Fa.reshape(B, S, F_COLS)
        Ga = Ga.reshape(B, S, G_COLS)

        wa, wb_, wk, wv = _mla_weights(mla_w_uq[l], mla_w_ukv[l])
        Q, K, V = _mla_prep(Fa, ctab, stab, mla_q_norm[l][None], mla_kv_norm[l][None], wa, wb_, wk, wv)
        ya = _flash(Q, K, V.transpose(0, 2, 1), npairs=4, t=t_big, shared_qk=False,
                    q_blk=((0, 2), (1, 2)), k_blk=((0, 2), (1, 2)), v_blk=(0, 1))

        sink = ((swa_sinks[l] - tab_swa[N_BUCKETS - 1]) * LOG2E)[PAIR_HEAD]
        sink = jnp.broadcast_to(jnp.pad(sink, ((0, 0), (0, 6)))[:, :, None], (4, 8, LANE)).astype(F32)
        GvT = Ga[:, :, GB_DV * LANE:].transpose(0, 2, 1)
        yb = _flash(Ga, Ga, GvT, npairs=4, t=t_att, window=SWA_WINDOW,
                    q_blk=(GB_SQ, 1), k_blk=(GB_SK, 0), v_blk=(GB_SV - GB_DV, 0), bias=bias_swa, sinks=sink)

        r2 = jnp.stack([Fa[:, :, FB_NKC * LANE:(FB_NKC + 1) * LANE],
                        Fa[:, :, FB_NVC * LANE:(FB_NVC + 1) * LANE]]).reshape(2, B, ncp, NSA_CMP_STRIDE * LANE)
        ptop, pbot, w1e, w2e = _compress_weights(nsa_cmp_pos[l], nsa_cmp_w1[l], nsa_cmp_w2[l])
        kv2 = _compress(r2, ptop, pbot, w1e, w2e)
        oc, sel = _cmp_attn(Ga, kv2, ovl)
        osel = _flash(Ga, Ga, GvT, npairs=4, t=t_big, q_blk=(GB_NQ, 1), k_blk=(GB_NKS, 0),
                      v_blk=(GB_NVS - GB_DV, 0), bias=bias_sel, sel=sel, emat=emat, out_dtype=F32)
        ow = _flash(Ga, Ga, GvT, npairs=4, t=t_att, window=NSA_WINDOW, q_blk=(GB_NQ, 1), k_blk=(GB_NKW, 0),
                    v_blk=(GB_NVW - GB_DV, 0), bias=bias_nsa, out_dtype=F32)

        lam_init = 0.8 - 0.6 * math.exp(-0.3 * l)
        lp = diff_lambda[l].astype(F32)
        lam = jnp.exp(jnp.sum(lp[0] * lp[1])) - jnp.exp(jnp.sum(lp[2] * lp[3])) + lam_init
        lam_arr = jnp.concatenate([jnp.full((1, LANE), 1.0, F32) * lam,
                                   jnp.full((7, LANE), 1.0 - lam_init, F32)], 0)
        yd = _flash(Ga, Ga, GvT, npairs=DIFF_HEADS, t=t_big, q_blk=(GB_DQ, 1), k_blk=(GB_DK, 1), v_blk=(0, 1),
                    bias=bias_diff, bias_shared=True, diff=(lam_arr, diff_subln[l][None]))

        wbr = jnp.stack([w_branch[l, 0], _pair_rows(w_branch[l, 1]), _pair_rows(w_branch[l, 2]),
                         w_branch[l, 3]]).astype(BF16)
        f2 = lambda a: a.reshape(T, a.shape[-1])
        h = _merge(h, gl[1], f2(ya), f2(yb), f2(oc), f2(osel), f2(ow), f2(Fa), f2(yd),
                   w_gate[l].astype(BF16), wbr, w_o[l].astype(BF16), p3)

        h = _ffn(h, gl[2], ffn_w_gate[l, 1].astype(BF16), ffn_w_up[l, 1].astype(BF16),
                 ffn_w_down[l, 1].astype(BF16), final_g[None], final=(l == depth - 1))
    return h.reshape(B, S, D)
```

```python
import functools
import math

import numpy as np
import jax
import jax.numpy as jnp
from jax import lax
from jax.experimental import pallas as pl
from jax.experimental.pallas import tpu as pltpu

F32 = jnp.float32
BF16 = jnp.bfloat16

D_MODEL = 1024
EPS = 1e-6
NEG = -1e30
D_FF = 2816

MLA_HEADS = 8
MLA_Q_RANK = 256
MLA_KV_RANK = 128
MLA_NOPE = 64
MLA_ROPE = 32
MLA_V = 64
ROPE_THETA = 10000.0

SWA_HEADS = 8
SWA_WINDOW = 128

NSA_HEADS = 8
NSA_HD = 64
NSA_CMP_LEN = 32
NSA_CMP_STRIDE = 16
NSA_CMP_HIDDEN = 128
NSA_SEL_LEN = 64
NSA_TOPK = 16
NSA_WINDOW = 512
NSA_FORCE = 1e4

DIFF_HEADS = 4
DIFF_HD = 64

N_BUCKETS = 32
MAX_DIST = 128

LANE = 128
HALF = 64
VMEM_LIMIT = 56 * 1024 * 1024
LOG2E = math.log2(math.e)
KEY_CHUNK = 256

O_CQ, O_CKV, O_KROPE, O_SQ, O_SK, O_SV, O_NQ = 0, 256, 384, 416, 928, 1056, 1184
O_NKC, O_NVC, O_NKS, O_NVS, O_NKW, O_NVW, O_NGATE = 1696, 1824, 1952, 2080, 2208, 2336, 2464
O_DQ, O_DK, O_DV = 2488, 3000, 3512

F_COLS = 1024
FB_CKV, FB_KROPE, FB_KSWAP, FB_NKC, FB_NVC, FB_NGATE = 2, 3, 4, 5, 6, 7
G_COLS = 3328
GB_SQ, GB_NQ, GB_DQ, GB_DK, GB_DV = 0, 4, 8, 12, 16
GB_SK, GB_SV, GB_NKS, GB_NVS, GB_NKW, GB_NVW = 20, 21, 22, 23, 24, 25


def _cparams(sem):
    return pltpu.CompilerParams(dimension_semantics=sem, vmem_limit_bytes=VMEM_LIMIT)


def _rms(x, g):
    return x * lax.rsqrt(jnp.mean(x * x, -1, keepdims=True) + EPS) * g


def _dot(a, b):
    return jnp.dot(a, b, preferred_element_type=F32)


def _dot_nt(a, b):
    return lax.dot_general(a, b, (((1,), (1,)), ((), ())), preferred_element_type=F32)


def _split3(x):
    hi = x.astype(BF16)
    r = x - hi.astype(F32)
    mid = r.astype(BF16)
    lo = (r - mid.astype(F32)).astype(BF16)
    return hi, mid, lo


def _dot_exact(x, w01):
    hi, mid, lo = _split3(x)
    return _dot(hi, w01) + _dot(mid, w01) + _dot(lo, w01)


def _ffn_kernel(h_ref, g_ref, wg_ref, wu_ref, wd_ref, gf_ref, o_ref, n_scr, acc_scr, *, final):
    j = pl.program_id(1)

    @pl.when(j == 0)
    def _():
        n_scr[...] = _rms(h_ref[...], g_ref[...]).astype(BF16)
        acc_scr[...] = jnp.zeros_like(acc_scr)

    n = n_scr[...]
    a = _dot(n, wg_ref[...])
    b = _dot(n, wu_ref[...])
    t = (a * jax.nn.sigmoid(a) * b).astype(BF16)
    acc_scr[...] += _dot(t, wd_ref[...])

    @pl.when(j == pl.num_programs(1) - 1)
    def _():
        out = h_ref[...] + 0.5 * acc_scr[...]
        if final:
            out = _rms(out, gf_ref[...])
        o_ref[...] = out


def _ffn(h, g, wg, wu, wd, gf, final):
    T = h.shape[0]
    tm, tf = 512, D_FF // 2
    return pl.pallas_call(
        functools.partial(_ffn_kernel, final=final),
        grid=(T // tm, D_FF // tf),
        in_specs=[
            pl.BlockSpec((tm, D_MODEL), lambda i, j: (i, 0)),
            pl.BlockSpec((1, D_MODEL), lambda i, j: (0, 0)),
            pl.BlockSpec((D_MODEL, tf), lambda i, j: (0, j)),
            pl.BlockSpec((D_MODEL, tf), lambda i, j: (0, j)),
            pl.BlockSpec((tf, D_MODEL), lambda i, j: (j, 0)),
            pl.BlockSpec((1, D_MODEL), lambda i, j: (0, 0)),
        ],
        out_specs=pl.BlockSpec((tm, D_MODEL), lambda i, j: (i, 0)),
        out_shape=jax.ShapeDtypeStruct((T, D_MODEL), F32),
        scratch_shapes=[pltpu.VMEM((tm, D_MODEL), BF16), pltpu.VMEM((tm, D_MODEL), F32)],
        compiler_params=_cparams(("parallel", "arbitrary")),
        name="ffn_half_step",
    )(h, g, wg, wu, wd, gf)


def _proj_kernel(h_ref, g_ref, w_ref, f_ref, gq_ref):
    u = _rms(h_ref[...], g_ref[...]).astype(BF16)
    f_ref[...] = _dot(u, w_ref[:, :F_COLS])
    step = 512
    for c0 in range(0, G_COLS, step):
        c1 = min(c0 + step, G_COLS)
        gq_ref[:, c0:c1] = _dot(u, w_ref[:, F_COLS + c0:F_COLS + c1]).astype(BF16)


def _proj(h, g, wp):
    T = h.shape[0]
    tm = 512
    return pl.pallas_call(
        _proj_kernel,
        grid=(T // tm,),
        in_specs=[
            pl.BlockSpec((tm, D_MODEL), lambda i: (i, 0)),
            pl.BlockSpec((1, D_MODEL), lambda i: (0, 0)),
            pl.BlockSpec((D_MODEL, F_COLS + G_COLS), lambda i: (0, 0)),
        ],
        out_specs=[
            pl.BlockSpec((tm, F_COLS), lambda i: (i, 0)),
            pl.BlockSpec((tm, G_COLS), lambda i: (i, 0)),
        ],
        out_shape=[jax.ShapeDtypeStruct((T, F_COLS), F32),
                   jax.ShapeDtypeStruct((T, G_COLS), BF16)],
        compiler_params=_cparams(("parallel",)),
        name="in_proj",
    )(h, g, wp)


def _proj_weight(w_in):
    ncol = F_COLS + G_COLS
    src = np.zeros((ncol,), np.int32)
    scale = np.zeros((ncol,), np.float32)

    def put(dst, srcs, s=1.0):
        srcs = np.asarray(srcs)
        src[dst:dst + len(srcs)] = srcs
        scale[dst:dst + len(srcs)] = s

    half = MLA_ROPE // 2
    put(0, O_CQ + np.arange(MLA_Q_RANK))
    put(FB_CKV * LANE, O_CKV + np.arange(MLA_KV_RANK))
    put(FB_KROPE * LANE + MLA_NOPE, O_KROPE + np.arange(MLA_ROPE))
    put(FB_KSWAP * LANE + MLA_NOPE, O_KROPE + np.concatenate([np.arange(half, MLA_ROPE), np.arange(half)]))
    put(FB_NKC * LANE, O_NKC + np.arange(LANE))
    put(FB_NVC * LANE, O_NVC + np.arange(LANE))
    put(FB_NGATE * LANE, O_NGATE + np.arange(3 * NSA_HEADS))
    g0 = F_COLS
    qscale = 0.125 * LOG2E
    pair_perm = np.concatenate([np.concatenate([(j) * HALF + np.arange(HALF), (4 + j) * HALF + np.arange(HALF)])
                                for j in range(4)])
    put(g0 + GB_SQ * LANE, O_SQ + pair_perm, qscale)
    put(g0 + GB_NQ * LANE, O_NQ + pair_perm, qscale)
    put(g0 + GB_DQ * LANE, O_DQ + np.arange(512), qscale)
    put(g0 + GB_DK * LANE, O_DK + np.arange(512))
    put(g0 + GB_DV * LANE, O_DV + np.arange(512))
    for blk, off in ((GB_SK, O_SK), (GB_SV, O_SV), (GB_NKS, O_NKS), (GB_NVS, O_NVS),
                     (GB_NKW, O_NKW), (GB_NVW, O_NVW)):
        put(g0 + blk * LANE, off + np.arange(LANE))
    return (w_in[:, src] * scale[None, :]).astype(BF16)


PAIR_HEAD = np.array([[j, 4 + j] for j in range(4)])


def _mla_prep_kernel(cq_ref, ckv_ref, kr_ref, ks_ref, ct_ref, st_ref, qg_ref, kg_ref,
                     wa_ref, wb_ref, wk_ref, wv_ref, q_ref, k_ref, v_ref, *, scale):
    qn = _rms(cq_ref[0], qg_ref[...]).astype(BF16)
    kn = _rms(ckv_ref[0], kg_ref[...]).astype(BF16)
    ct = ct_ref[...]
    st = st_ref[...]
    krot = kr_ref[0] * ct + ks_ref[0] * st
    v_ref[0] = _dot(kn, wv_ref[...]).astype(BF16)
    for h in range(MLA_HEADS):
        sl = slice(h * LANE, (h + 1) * LANE)
        qa = _dot(qn, wa_ref[:, sl])
        qb = _dot(qn, wb_ref[:, sl])
        q_ref[0, :, sl] = ((qa * ct + qb * st) * scale).astype(BF16)
        k_ref[0, :, sl] = (_dot(kn, wk_ref[:, sl]) + krot).astype(BF16)


def _mla_prep(Fa, ctab, stab, qg, kg, wa, wb, wk, wv):
    B, S, _ = Fa.shape
    tm = 512
    HW = MLA_HEADS * LANE
    full = lambda shape: pl.BlockSpec(shape, lambda b, i: (0,) * len(shape))
    return pl.pallas_call(
        functools.partial(_mla_prep_kernel, scale=(MLA_NOPE + MLA_ROPE) ** -0.5 * LOG2E),
        grid=(B, S // tm),
        in_specs=[
            pl.BlockSpec((1, tm, MLA_Q_RANK), lambda b, i: (b, i, 0)),
            pl.BlockSpec((1, tm, LANE), lambda b, i: (b, i, FB_CKV)),
            pl.BlockSpec((1, tm, LANE), lambda b, i: (b, i, FB_KROPE)),
            pl.BlockSpec((1, tm, LANE), lambda b, i: (b, i, FB_KSWAP)),
            pl.BlockSpec((tm, LANE), lambda b, i: (i, 0)),
            pl.BlockSpec((tm, LANE), lambda b, i: (i, 0)),
            full((1, MLA_Q_RANK)), full((1, MLA_KV_RANK)),
            full((MLA_Q_RANK, HW)), full((MLA_Q_RANK, HW)),
            full((MLA_KV_RANK, HW)), full((MLA_KV_RANK, MLA_HEADS * MLA_V)),
        ],
        out_specs=[
            pl.BlockSpec((1, tm, HW), lambda b, i: (b, i, 0)),
            pl.BlockSpec((1, tm, HW), lambda b, i: (b, i, 0)),
            pl.BlockSpec((1, tm, MLA_HEADS * MLA_V), lambda b, i: (b, i, 0)),
        ],
        out_shape=[jax.ShapeDtypeStruct((B, S, HW), BF16),
                   jax.ShapeDtypeStruct((B, S, HW), BF16),
                   jax.ShapeDtypeStruct((B, S, MLA_HEADS * MLA_V), BF16)],
        compiler_params=_cparams(("parallel", "parallel")),
        name="mla_prep",
    )(Fa, Fa, Fa, Fa, ctab, stab, qg, kg, wa, wb, wk, wv)


def _mla_weights(w_uq, w_ukv):
    dq = MLA_NOPE + MLA_ROPE
    half = MLA_ROPE // 2
    z = lambda n: jnp.zeros((w_uq.shape[0], n), F32)
    wa, wb, wk, wv = [], [], [], []
    for h in range(MLA_HEADS):
        nope = w_uq[:, h * dq:h * dq + MLA_NOPE]
        r = w_uq[:, h * dq + MLA_NOPE:(h + 1) * dq]
        wa += [nope, r, z(LANE - dq)]
        wb += [z(MLA_NOPE), r[:, half:], r[:, :half], z(LANE - dq)]
        wk += [w_ukv[:, h * LANE:h * LANE + MLA_NOPE], jnp.zeros((w_ukv.shape[0], LANE - MLA_NOPE), F32)]
        wv += [w_ukv[:, h * LANE + MLA_NOPE:(h + 1) * LANE]]
    cat = lambda xs: jnp.concatenate(xs, axis=1).astype(BF16)
    return cat(wa), cat(wb), cat(wk), cat(wv)


def _rope_tables(S):
    half = MLA_ROPE // 2
    freqs = ROPE_THETA ** (-jnp.arange(half, dtype=F32) / half)
    ang = jnp.arange(S, dtype=jnp.int32)[:, None].astype(F32) * freqs
    cos, sin = jnp.cos(ang), jnp.sin(ang)
    ones = jnp.ones((S, MLA_NOPE), F32)
    zeros = jnp.zeros((S, LANE - MLA_NOPE - MLA_ROPE), F32)
    ctab = jnp.concatenate([ones, cos, cos, zeros], 1)
    stab = jnp.concatenate([jnp.zeros((S, MLA_NOPE), F32), -sin, sin, zeros], 1)
    return ctab, stab


def _t5_bucket(dist):
    n = jnp.maximum(dist, 0)
    max_exact = N_BUCKETS // 2
    nf = jnp.maximum(n, 1).astype(F32)
    large = max_exact + (jnp.log(nf / max_exact) / math.log(MAX_DIST / max_exact)
                         * (N_BUCKETS - max_exact)).astype(jnp.int32)
    large = jnp.minimum(large, N_BUCKETS - 1)
    return jnp.where(n < max_exact, n, large)


def _bias_tiles(tab, t):
    assert t >= MAX_DIST
    i = jnp.arange(t, dtype=jnp.int32)[:, None]
    j = jnp.arange(t, dtype=jnp.int32)[None, :]
    tiles = []
    tab_t = tab.T[:, :, None, None]
    for rel in range(2):
        b = _t5_bucket(rel * t + j - i)[None]
        tile = jnp.zeros((tab.shape[1], t, t), F32)
        for n in range(N_BUCKETS):
            tile = jnp.where(b == n, tab_t[:, n], tile)
        tiles.append(tile)
    far = tab[N_BUCKETS - 1][:, None, None, None]
    return (jnp.stack(tiles, 1) - far) * LOG2E


def _flash_kernel(*refs, t, kc, window, shared_qk, has_bias, bias_shared, has_sel, has_sink, diff, out_dtype):
    qt_ref, kt_ref, first_ref, last_ref, rel_ref = refs[:5]
    refs = list(refs[5:])
    if shared_qk:
        q_ref, k_ref = refs[:2]
        refs = refs[2:]
    else:
        qa_ref, qb_ref, ka_ref, kb_ref = refs[:4]
        refs = refs[4:]
    vt_ref = refs.pop(0)
    if has_bias:
        ba_ref = refs.pop(0)
        bb_ref = ba_ref if bias_shared else refs.pop(0)
    if has_sel:
        sa_ref, sb_ref, et_ref = refs[:3]
        refs = refs[3:]
    if has_sink:
        sink_ref = refs.pop(0)
    if diff:
        lam_ref, sub_ref = refs[:2]
        refs = refs[2:]
    o_ref, m_scr, l_scr, acc_scr, p_scr = refs

    step = pl.program_id(2)
    qt = qt_ref[step]
    kt = kt_ref[step]
    lo = lax.broadcasted_iota(jnp.int32, (1, LANE), 1) < HALF

    @pl.when(first_ref[step] == 1)
    def _():
        for a in range(2):
            if has_sink:
                m_scr[a] = jnp.broadcast_to(sink_ref[0, a:a + 1, 0:1], (1, t))
                l_scr[a] = jnp.ones((1, t), F32)
            else:
                m_scr[a] = jnp.full((1, t), NEG, F32)
                l_scr[a] = jnp.zeros((1, t), F32)
            acc_scr[a] = jnp.zeros((LANE, t), F32)

    if shared_qk:
        q = q_ref[0]
        zero = jnp.zeros_like(q)
        qs = (jnp.where(lo, q, zero), jnp.where(lo, zero, q))
        ks = (k_ref[0], k_ref[0])
    else:
        qs = (qa_ref[0], qb_ref[0])
        ks = (ka_ref[0], kb_ref[0])
    vt = vt_ref[0]

    def process(rel):
        col_slices = [slice(c, c + LANE) for c in range(0, t, LANE)]

        def unit_range(kh, c):
            base = rel * t + c - kh
            return base, base - (kc - 1), base + (LANE - 1)

        def unit_dead(kh, c):
            _, dmin, dmax = unit_range(kh, c)
            return (rel == 0 and dmax < 0) or (window is not None and dmin >= window)

        jobs = [(kh, a) for kh in range(0, t, kc) for a in range(2)
                if not all(unit_dead(kh, cs.start) for cs in col_slices)]

        def scores(job):
            kh, a = job
            kk, qq = ks[a][kh:kh + kc], qs[a]
            if has_sel:
                kk = jnp.concatenate([kk, et_ref[kh:kh + kc, :]], axis=1)
                qq = jnp.concatenate([qq, (sa_ref if a == 0 else sb_ref)[0]], axis=1)
            return _dot_nt(kk, qq)

        m = [[m_scr[a, :, cs] for cs in col_slices] for a in range(2)]
        l = [[l_scr[a, :, cs] for cs in col_slices] for a in range(2)]
        acc = [acc_scr[0], acc_scr[1]]
        nxt = scores(jobs[0])
        for ji, (kh, a) in enumerate(jobs):
            keys = slice(kh, kh + kc)
            st = nxt
            if ji + 1 < len(jobs):
                nxt = scores(jobs[ji + 1])
            b_ref = (ba_ref if a == 0 else bb_ref) if has_bias else None
            alphas = []
            for ci, cols in enumerate(col_slices):
                base, dmin, dmax = unit_range(kh, cols.start)
                if unit_dead(kh, cols.start):
                    alphas.append(jnp.ones((1, LANE), F32))
                    p_scr[a, keys, cols] = jnp.zeros((kc, LANE), BF16)
                    continue
                s = st[:, cols]
                if has_bias and rel < 2 and dmin < MAX_DIST:
                    s = s + b_ref[0, 0, 0, keys, cols]
                need_lo = rel == 0 and dmin < 0
                need_hi = window is not None and dmax >= window
                if need_lo or need_hi:
                    d = (lax.broadcasted_iota(jnp.int32, (kc, LANE), 1)
                         - lax.broadcasted_iota(jnp.int32, (kc, LANE), 0)) + base
                    valid = (d >= 0) if need_lo else None
                    if need_hi:
                        valid = (d < window) if valid is None else (valid & (d < window))
                    s = jnp.where(valid, s, NEG)
                m_prev = m[a][ci]
                m_new = jnp.maximum(m_prev, jnp.max(s, 0, keepdims=True))
                alpha = jnp.exp2(m_prev - m_new)
                p = jnp.exp2(s - m_new)
                l[a][ci] = alpha * l[a][ci] + jnp.sum(p, 0, keepdims=True)
                m[a][ci] = m_new
                alphas.append(alpha)
                p_scr[a, keys, cols] = p.astype(BF16)
            acc[a] = jnp.concatenate(alphas, axis=1) * acc[a] + _dot(vt[:, keys], p_scr[a, keys, :])
        for a in range(2):
            for ci, cs in enumerate(col_slices):
                m_scr[a, :, cs] = m[a][ci]
                l_scr[a, :, cs] = l[a][ci]
            acc_scr[a] = acc[a]

    for r in range(2 if window is not None else 3):
        @pl.when(rel_ref[step] == r)
        def _(r=r):
            process(r)

    @pl.when(last_ref[step] == 1)
    def _():
        o0 = acc_scr[0] / l_scr[0]
        o1 = acc_scr[1] / l_scr[1]
        if diff:
            o = (o0 - lam_ref[0:1, 0:1] * o1).T
            o = _rms(o, sub_ref[...]) * lam_ref[1:2, 0:1]
        else:
            feat = lax.broadcasted_iota(jnp.int32, (LANE, 1), 0)
            o = jnp.where(feat < HALF, o0, o1).T
        o_ref[0] = o.astype(out_dtype)


def _flash(q_arr, k_arr, vt_arr, *, npairs, t, q_blk, k_blk, v_blk, window=None, shared_qk=True,
           bias=None, bias_shared=False, sel=None, emat=None, sinks=None, diff=None, out_dtype=BF16):
    B, S, _ = q_arr.shape
    nq = S // t
    nprev = None if window is None else -(-(window - 1) // t)
    assert window is None or window <= t + 1, "a window must end in the previous tile"
    steps = []
    for qi in range(nq):
        k0 = 0 if nprev is None else max(0, qi - nprev)
        for ki in range(k0, qi + 1):
            steps.append((qi, ki, int(ki == k0), int(ki == qi), min(qi - ki, 2)))
    tabs = [jnp.asarray(np.array([s[c] for s in steps], np.int32)) for c in range(5)]
    nsteps = len(steps)
    kc = min(t, KEY_CHUNK)

    def qmap(off, stride):
        return lambda b, p, s, qt, kt, fr, la, rl: (b, qt[s], off + stride * p)

    def kmap(off, stride):
        return lambda b, p, s, qt, kt, fr, la, rl: (b, kt[s], off + stride * p)

    in_specs, args = [], []
    if shared_qk:
        in_specs += [pl.BlockSpec((1, t, LANE), qmap(*q_blk)), pl.BlockSpec((1, t, LANE), kmap(*k_blk))]
        args += [q_arr, k_arr]
    else:
        in_specs += [pl.BlockSpec((1, t, LANE), qmap(*q_blk[0])), pl.BlockSpec((1, t, LANE), qmap(*q_blk[1])),
                     pl.BlockSpec((1, t, LANE), kmap(*k_blk[0])), pl.BlockSpec((1, t, LANE), kmap(*k_blk[1]))]
        args += [q_arr, q_arr, k_arr, k_arr]
    voff, vstride = v_blk
    in_specs.append(pl.BlockSpec((1, LANE, t),
                                 lambda b, p, s, qt, kt, fr, la, rl: (b, voff + vstride * p, kt[s])))
    args.append(vt_arr)
    if bias is not None:
        for a in range(1 if bias_shared else 2):
            in_specs.append(pl.BlockSpec((1, 1, 1, t, t),
                                         lambda b, p, s, qt, kt, fr, la, rl, a=a: (a, p, jnp.minimum(rl[s], 1), 0, 0)))
            args.append(bias)
    if sel is not None:
        nselp = emat.shape[1]
        for a in range(2):
            in_specs.append(pl.BlockSpec((1, t, nselp),
                                         lambda b, p, s, qt, kt, fr, la, rl, a=a: (b, qt[s], a)))
            args.append(sel)
        in_specs.append(pl.BlockSpec((t, nselp), lambda b, p, s, qt, kt, fr, la, rl: (kt[s], 0)))
        args.append(emat)
    if sinks is not None:
        in_specs.append(pl.BlockSpec((1, 8, LANE), lambda b, p, s, qt, kt, fr, la, rl: (p, 0, 0)))
        args.append(sinks)
    if diff is not None:
        lam_arr, sub = diff
        in_specs.append(pl.BlockSpec((8, LANE), lambda b, p, s, qt, kt, fr, la, rl: (0, 0)))
        in_specs.append(pl.BlockSpec((1, LANE), lambda b, p, s, qt, kt, fr, la, rl: (0, 0)))
        args += [lam_arr, sub]

    kern = functools.partial(
        _flash_kernel, t=t, kc=kc, window=window, shared_qk=shared_qk, has_bias=bias is not None,
        bias_shared=bias_shared, has_sel=sel is not None, has_sink=sinks is not None,
        diff=diff is not None, out_dtype=out_dtype)
    return pl.pallas_call(
        kern,
        grid_spec=pltpu.PrefetchScalarGridSpec(
            num_scalar_prefetch=5,
            grid=(B, npairs, nsteps),
            in_specs=in_specs,
            out_specs=pl.BlockSpec((1, t, LANE), lambda b, p, s, qt, kt, fr, la, rl: (b, qt[s], p)),
            scratch_shapes=[pltpu.VMEM((2, 1, t), F32), pltpu.VMEM((2, 1, t), F32),
                            pltpu.VMEM((2, LANE, t), F32), pltpu.VMEM((2, t, t), BF16)],
        ),
        out_shape=jax.ShapeDtypeStruct((B, S, npairs * LANE), out_dtype),
        compiler_params=_cparams(("parallel", "parallel", "arbitrary")),
        name="flash_pairs",
    )(*tabs, *args)


def _compress_kernel(r_ref, pt_ref, pb_ref, w1_ref, w2_ref, o_ref):
    r = r_ref[0, 0]
    n = r.shape[0]
    xt = (r + pt_ref[0]).astype(BF16)
    xb = (r + pb_ref[0]).astype(BF16)
    out = jnp.zeros((n, LANE), F32)
    for h in range(2):
        top = _dot(xt, w1_ref[0, h, 0])
        bot = _dot(xb, w1_ref[0, h, 1])
        x = top + pltpu.roll(bot, n - 1, 0)
        hd = 0.5 * x * (1.0 + jnp.tanh(math.sqrt(2.0 / math.pi) * (x + 0.044715 * (x * x * x))))
        out = out + _dot(hd.astype(BF16), w2_ref[0, h])
    o_ref[0, 0] = out.astype(BF16)


def _compress(r2, ptop, pbot, w1e, w2e):
    _, B, n, K = r2.shape
    return pl.pallas_call(
        _compress_kernel,
        grid=(2, B),
        in_specs=[
            pl.BlockSpec((1, 1, n, K), lambda c, b: (c, b, 0, 0)),
            pl.BlockSpec((1, 1, K), lambda c, b: (c, 0, 0)),
            pl.BlockSpec((1, 1, K), lambda c, b: (c, 0, 0)),
            pl.BlockSpec((1, 2, 2, K, NSA_CMP_HIDDEN), lambda c, b: (c, 0, 0, 0, 0)),
            pl.BlockSpec((1, 2, NSA_CMP_HIDDEN, LANE), lambda c, b: (c, 0, 0, 0)),
        ],
        out_specs=pl.BlockSpec((1, 1, n, LANE), lambda c, b: (c, b, 0, 0)),
        out_shape=jax.ShapeDtypeStruct((2, B, n, LANE), BF16),
        compiler_params=_cparams(("parallel", "parallel")),
        name="nsa_compress",
    )(r2, ptop, pbot, w1e, w2e)


def _compress_weights(cmp_pos, cmp_w1, cmp_w2):
    L2 = NSA_CMP_STRIDE
    d = NSA_HD
    w1 = cmp_w1.reshape(2, 2, L2, d, NSA_CMP_HIDDEN)
    z = jnp.zeros_like(w1)
    e0 = jnp.concatenate([w1, z], axis=3)
    e1 = jnp.concatenate([z, w1], axis=3)
    w1e = jnp.stack([e0, e1], axis=1).reshape(2, 2, 2, L2 * 2 * d, NSA_CMP_HIDDEN).astype(BF16)
    pos = cmp_pos.reshape(2, 2, L2, 1, d)
    pos = jnp.broadcast_to(pos, (2, 2, L2, 2, d)).reshape(2, 2, 1, L2 * 2 * d)
    zw = jnp.zeros_like(cmp_w2)
    w2e = jnp.stack([jnp.concatenate([cmp_w2, zw], -1), jnp.concatenate([zw, cmp_w2], -1)], 1).astype(BF16)
    return pos[:, 0], pos[:, 1], w1e, w2e


def _cmp_attn_kernel(q_ref, kc_ref, vc_ref, ov_ref, oc_ref, sel_ref, *, tq, ncp, nselp):
    i = pl.program_id(1)
    qpos = i * tq + lax.broadcasted_iota(jnp.int32, (tq, 1), 0)
    cidx = lax.broadcasted_iota(jnp.int32, (1, ncp), 1)
    cvalid = (cidx * NSA_CMP_STRIDE + (NSA_CMP_LEN - 1)) <= qpos
    lane = lax.broadcasted_iota(jnp.int32, (1, LANE), 1)
    lo = lane < HALF
    kc = kc_ref[0, 0]
    vc = vc_ref[0, 0]
    psum = [jnp.zeros((tq, ncp), F32), jnp.zeros((tq, ncp), F32)]
    for j in range(4):
        q2 = q_ref[0, :, j * LANE:(j + 1) * LANE]
        zero = jnp.zeros_like(q2)
        outs = []
        for a in range(2):
            qa = jnp.where(lo, q2, zero) if a == 0 else jnp.where(lo, zero, q2)
            s = jnp.where(cvalid, _dot_nt(qa, kc), NEG)
            m = jnp.max(s, -1, keepdims=True)
            e = jnp.where(cvalid, jnp.exp2(s - m), 0.0)
            den = jnp.sum(e, -1, keepdims=True)
            p = e / jnp.where(den > 0, den, 1.0)
            psum[a] = psum[a] + p
            outs.append(_dot(p.astype(BF16), vc))
        oc_ref[0, :, j * LANE:(j + 1) * LANE] = jnp.where(lo, outs[0], outs[1])

    n = lax.broadcasted_iota(jnp.int32, (1, nselp), 1)
    cur = qpos >> int(math.log2(NSA_SEL_LEN))
    causal = (n * NSA_SEL_LEN) <= qpos
    forced = (n == 0) | (n == cur) | (n == cur - 1)
    nf = n.astype(F32)
    for a in range(2):
        imp = _dot_exact(psum[a], ov_ref[...])
        x = jnp.where(causal, jnp.where(forced, NSA_FORCE, imp), -1.0)
        picked = jnp.zeros((tq, nselp), F32)
        for _ in range(NSA_TOPK):
            mx = jnp.max(x, -1, keepdims=True)
            idx = jnp.min(jnp.where(x == mx, nf, float(nselp)), -1, keepdims=True)
            hit = nf == idx
            picked = jnp.where(hit, 1.0, picked)
            x = jnp.where(hit, -2.0, x)
        sel_ref[0, :, a * nselp:(a + 1) * nselp] = jnp.where(causal & (picked > 0.5), 0.0, NEG).astype(BF16)


def _cmp_attn(G3, kv2, ovl):
    B, S, _ = G3.shape
    ncp = kv2.shape[2]
    nselp = ovl.shape[1]
    tq = 512
    return pl.pallas_call(
        functools.partial(_cmp_attn_kernel, tq=tq, ncp=ncp, nselp=nselp),
        grid=(B, S // tq),
        in_specs=[
            pl.BlockSpec((1, tq, 4 * LANE), lambda b, i: (b, i, GB_NQ // 4)),
            pl.BlockSpec((1, 1, ncp, LANE), lambda b, i: (0, b, 0, 0)),
            pl.BlockSpec((1, 1, ncp, LANE), lambda b, i: (1, b, 0, 0)),
            pl.BlockSpec((ncp, nselp), lambda b, i: (0, 0)),
        ],
        out_specs=[
            pl.BlockSpec((1, tq, 4 * LANE), lambda b, i: (b, i, 0)),
            pl.BlockSpec((1, tq, 2 * nselp), lambda b, i: (b, i, 0)),
        ],
        out_shape=[jax.ShapeDtypeStruct((B, S, 4 * LANE), F32),
                   jax.ShapeDtypeStruct((B, S, 2 * nselp), BF16)],
        compiler_params=_cparams(("parallel", "parallel")),
        name="nsa_cmp_attn_topk",
    )(G3, kv2, kv2, ovl)


def _merge_kernel(h_ref, g_ref, ya_ref, yb_ref, oc_ref, os_ref, ow_ref, ng_ref, yd_ref,
                  wg_ref, wb_ref, wo_ref, p3_ref, o_ref):
    h = h_ref[...]
    u = _rms(h, g_ref[...]).astype(BF16)
    gexp = _dot_exact(jax.nn.sigmoid(ng_ref[...]), p3_ref[...])
    W = 4 * LANE
    yc = gexp[:, :W] * oc_ref[...] + gexp[:, W:2 * W] * os_ref[...] + gexp[:, 2 * W:] * ow_ref[...]
    ys = (ya_ref[...], yb_ref[...], yc.astype(BF16), yd_ref[...])
    merged = jnp.zeros(h.shape, F32)
    for i in range(4):
        gate = jax.nn.sigmoid(_dot(u, wg_ref[i]))
        merged = merged + gate * _dot(ys[i], wb_ref[i])
    o_ref[...] = h + _dot(merged.astype(BF16), wo_ref[...])


def _merge(h, g, ya, yb, oc, osel, ow, Fa, yd, wg, wb, wo, p3):
    T = h.shape[0]
    tm = 256
    W = 4 * LANE
    row = lambda w: pl.BlockSpec((tm, w), lambda i: (i, 0))
    const = lambda shape: pl.BlockSpec(shape, lambda i: (0,) * len(shape))
    return pl.pallas_call(
        _merge_kernel,
        grid=(T // tm,),
        in_specs=[
            row(D_MODEL), const((1, D_MODEL)), row(W), row(W), row(W), row(W), row(W),
            pl.BlockSpec((tm, LANE), lambda i: (i, FB_NGATE)), row(W),
            const((4, D_MODEL, D_MODEL)), const((4, W, D_MODEL)), const((D_MODEL, D_MODEL)),
            const((LANE, 3 * W)),
        ],
        out_specs=row(D_MODEL),
        out_shape=jax.ShapeDtypeStruct((T, D_MODEL), F32),
        compiler_params=_cparams(("parallel",)),
        name="gated_merge",
    )(h, g, ya, yb, oc, osel, ow, Fa, yd, wg, wb, wo, p3)


def _gate_expand_matrix():
    W = 4 * LANE
    m = np.zeros((LANE, 3 * W), np.float32)
    for j in range(4):
        for a in range(2):
            head = PAIR_HEAD[j, a]
            for r in range(3):
                c0 = r * W + j * LANE + a * HALF
                m[head * 3 + r, c0:c0 + HALF] = 1.0
    return jnp.asarray(m, BF16)


def _pair_rows(w):
    perm = np.concatenate([np.arange(HALF) + PAIR_HEAD[j, a] * HALF for j in range(4) for a in range(2)])
    return w[perm]


def kernel(x, norm_g, w_in, mla_q_norm, mla_kv_norm, mla_w_uq, mla_w_ukv, swa_sinks,
           nsa_cmp_pos, nsa_cmp_w1, nsa_cmp_w2, diff_lambda, diff_subln, rel_bias_table,
           w_branch, w_gate, w_o, ffn_w_gate, ffn_w_up, ffn_w_down, final_g):
    B, S, D = x.shape
    T = B * S
    depth = w_in.shape[0]
    t_big = min(S, 1024)
    ncp = S // NSA_CMP_STRIDE
    nsel = S // NSA_SEL_LEN
    nselp = -(-nsel // LANE) * LANE

    tab_swa = rel_bias_table[:, :SWA_HEADS]
    tab_nsa = rel_bias_table[:, SWA_HEADS:SWA_HEADS + NSA_HEADS]
    tab_diff = rel_bias_table[:, SWA_HEADS + NSA_HEADS:]
    pair_heads = PAIR_HEAD.T.reshape(-1)
    bias_swa = _bias_tiles(tab_swa[:, pair_heads], t_big).reshape(2, 4, 2, t_big, t_big)
    bias_sel = _bias_tiles(tab_nsa[:, pair_heads], t_big).reshape(2, 4, 2, t_big, t_big)
    bias_diff = _bias_tiles(tab_diff, t_big)[None]
    ctab, stab = _rope_tables(S)

    ci = np.arange(ncp)[:, None]
    ni = np.arange(nselp)[None, :]
    nc_real = (S - NSA_CMP_LEN) // NSA_CMP_STRIDE + 1
    ovl = ((ci * NSA_CMP_STRIDE < ni * NSA_SEL_LEN + NSA_SEL_LEN)
           & (ni * NSA_SEL_LEN < ci * NSA_CMP_STRIDE + NSA_CMP_LEN) & (ci < nc_real) & (ni < nsel))
    ovl = jnp.asarray(ovl.astype(np.float32), BF16)
    emat = jnp.asarray((np.arange(S)[:, None] // NSA_SEL_LEN == np.arange(nselp)[None, :]).astype(np.float32), BF16)
    p3 = _gate_expand_matrix()

    h = x.reshape(T, D)
    for l in range(depth):
        gl = norm_g[l].reshape(3, 1, D)
        h = _ffn(h, gl[0], ffn_w_gate[l, 0].astype(BF16), ffn_w_up[l, 0].astype(BF16),
                 ffn_w_down[l, 0].astype(BF16), gl[0], final=False)

        Fa, Ga = _proj(h, gl[1], _proj_weight(w_in[l]))
        Fa = Fa.reshape(B, S, F_COLS)
        Ga = Ga.reshape(B, S, G_COLS)

        wa, wb_, wk, wv = _mla_weights(mla_w_uq[l], mla_w_ukv[l])
        Q, K, V = _mla_prep(Fa, ctab, stab, mla_q_norm[l][None], mla_kv_norm[l][None], wa, wb_, wk, wv)
        ya = _flash(Q, K, V.transpose(0, 2, 1), npairs=4, t=t_big, shared_qk=False,
                    q_blk=((0, 2), (1, 2)), k_blk=((0, 2), (1, 2)), v_blk=(0, 1))

        sink = ((swa_sinks[l] - tab_swa[N_BUCKETS - 1]) * LOG2E)[PAIR_HEAD]
        sink = jnp.broadcast_to(jnp.pad(sink, ((0, 0), (0, 6)))[:, :, None], (4, 8, LANE)).astype(F32)
        GvT = Ga[:, :, GB_DV * LANE:].transpose(0, 2, 1)
        yb = _flash(Ga, Ga, GvT, npairs=4, t=t_big, window=SWA_WINDOW,
                    q_blk=(GB_SQ, 1), k_blk=(GB_SK, 0), v_blk=(GB_SV - GB_DV, 0), bias=bias_swa, sinks=sink)

        r2 = jnp.stack([Fa[:, :, FB_NKC * LANE:(FB_NKC + 1) * LANE],
                        Fa[:, :, FB_NVC * LANE:(FB_NVC + 1) * LANE]]).reshape(2, B, ncp, NSA_CMP_STRIDE * LANE)
        ptop, pbot, w1e, w2e = _compress_weights(nsa_cmp_pos[l], nsa_cmp_w1[l], nsa_cmp_w2[l])
        kv2 = _compress(r2, ptop, pbot, w1e, w2e)
        oc, sel = _cmp_attn(Ga, kv2, ovl)
        osel = _flash(Ga, Ga, GvT, npairs=4, t=t_big, q_blk=(GB_NQ, 1), k_blk=(GB_NKS, 0),
                      v_blk=(GB_NVS - GB_DV, 0), bias=bias_sel, sel=sel, emat=emat, out_dtype=F32)
        ow = _flash(Ga, Ga, GvT, npairs=4, t=t_big, window=NSA_WINDOW, q_blk=(GB_NQ, 1), k_blk=(GB_NKW, 0),
                    v_blk=(GB_NVW - GB_DV, 0), bias=bias_sel, out_dtype=F32)

        lam_init = 0.8 - 0.6 * math.exp(-0.3 * l)
        lp = diff_lambda[l].astype(F32)
        lam = jnp.exp(jnp.sum(lp[0] * lp[1])) - jnp.exp(jnp.sum(lp[2] * lp[3])) + lam_init
        lam_arr = jnp.concatenate([jnp.full((1, LANE), 1.0, F32) * lam,
                                   jnp.full((7, LANE), 1.0 - lam_init, F32)], 0)
        yd = _flash(Ga, Ga, GvT, npairs=DIFF_HEADS, t=t_big, q_blk=(GB_DQ, 1), k_blk=(GB_DK, 1), v_blk=(0, 1),
                    bias=bias_diff, bias_shared=True, diff=(lam_arr, diff_subln[l][None]))

        wbr = jnp.stack([w_branch[l, 0], _pair_rows(w_branch[l, 1]), _pair_rows(w_branch[l, 2]),
                         w_branch[l, 3]]).astype(BF16)
        f2 = lambda a: a.reshape(T, a.shape[-1])
        h = _merge(h, gl[1], f2(ya), f2(yb), f2(oc), f2(osel), f2(ow), f2(Fa), f2(yd),
                   w_gate[l].astype(BF16), wbr, w_o[l].astype(BF16), p3)

        h = _ffn(h, gl[2], ffn_w_gate[l, 1].astype(BF16), ffn_w_up[l, 1].astype(BF16),
                 ffn_w_down[l, 1].astype(BF16), final_g[None], final=(l == depth - 1))
    return h.reshape(B, S, D)
```

```python
import functools
import math

import numpy as np
import jax
import jax.numpy as jnp
from jax import lax
from jax.experimental import pallas as pl
from jax.experimental.pallas import tpu as pltpu

F32 = jnp.float32
BF16 = jnp.bfloat16

D_MODEL = 1024
EPS = 1e-6
NEG = -1e30
D_FF = 2816

MLA_HEADS = 8
MLA_Q_RANK = 256
MLA_KV_RANK = 128
MLA_NOPE = 64
MLA_ROPE = 32
MLA_V = 64
ROPE_THETA = 10000.0

SWA_HEADS = 8
SWA_WINDOW = 128

NSA_HEADS = 8
NSA_HD = 64
NSA_CMP_LEN = 32
NSA_CMP_STRIDE = 16
NSA_CMP_HIDDEN = 128
NSA_SEL_LEN = 64
NSA_TOPK = 16
NSA_WINDOW = 512
NSA_FORCE = 1e4

DIFF_HEADS = 4
DIFF_HD = 64

N_BUCKETS = 32
MAX_DIST = 128

LANE = 128
HALF = 64
VMEM_LIMIT = 56 * 1024 * 1024
LOG2E = math.log2(math.e)
KEY_CHUNK = 256

O_CQ, O_CKV, O_KROPE, O_SQ, O_SK, O_SV, O_NQ = 0, 256, 384, 416, 928, 1056, 1184
O_NKC, O_NVC, O_NKS, O_NVS, O_NKW, O_NVW, O_NGATE = 1696, 1824, 1952, 2080, 2208, 2336, 2464
O_DQ, O_DK, O_DV = 2488, 3000, 3512

F_COLS = 1024
FB_CKV, FB_KROPE, FB_KSWAP, FB_NKC, FB_NVC, FB_NGATE = 2, 3, 4, 5, 6, 7
G_COLS = 3328
GB_SQ, GB_NQ, GB_DQ, GB_DK, GB_DV = 0, 4, 8, 12, 16
GB_SK, GB_SV, GB_NKS, GB_NVS, GB_NKW, GB_NVW = 20, 21, 22, 23, 24, 25


def _cparams(sem):
    return pltpu.CompilerParams(dimension_semantics=sem, vmem_limit_bytes=VMEM_LIMIT)


def _rms(x, g):
    return x * lax.rsqrt(jnp.mean(x * x, -1, keepdims=True) + EPS) * g


def _dot(a, b):
    return jnp.dot(a, b, preferred_element_type=F32)


def _dot_nt(a, b):
    return lax.dot_general(a, b, (((1,), (1,)), ((), ())), preferred_element_type=F32)


def _split3(x):
    hi = x.astype(BF16)
    r = x - hi.astype(F32)
    mid = r.astype(BF16)
    lo = (r - mid.astype(F32)).astype(BF16)
    return hi, mid, lo


def _dot_exact(x, w01):
    hi, mid, lo = _split3(x)
    return _dot(hi, w01) + _dot(mid, w01) + _dot(lo, w01)


def _ffn_kernel(h_ref, g_ref, wg_ref, wu_ref, wd_ref, gf_ref, o_ref, n_scr, acc_scr, *, final):
    j = pl.program_id(1)

    @pl.when(j == 0)
    def _():
        n_scr[...] = _rms(h_ref[...], g_ref[...]).astype(BF16)
        acc_scr[...] = jnp.zeros_like(acc_scr)

    n = n_scr[...]
    a = _dot(n, wg_ref[...])
    b = _dot(n, wu_ref[...])
    t = (a * jax.nn.sigmoid(a) * b).astype(BF16)
    acc_scr[...] += _dot(t, wd_ref[...])

    @pl.when(j == pl.num_programs(1) - 1)
    def _():
        out = h_ref[...] + 0.5 * acc_scr[...]
        if final:
            out = _rms(out, gf_ref[...])
        o_ref[...] = out


def _ffn(h, g, wg, wu, wd, gf, final):
    T = h.shape[0]
    tm, tf = 512, D_FF // 2
    return pl.pallas_call(
        functools.partial(_ffn_kernel, final=final),
        grid=(T // tm, D_FF // tf),
        in_specs=[
            pl.BlockSpec((tm, D_MODEL), lambda i, j: (i, 0)),
            pl.BlockSpec((1, D_MODEL), lambda i, j: (0, 0)),
            pl.BlockSpec((D_MODEL, tf), lambda i, j: (0, j)),
            pl.BlockSpec((D_MODEL, tf), lambda i, j: (0, j)),
            pl.BlockSpec((tf, D_MODEL), lambda i, j: (j, 0)),
            pl.BlockSpec((1, D_MODEL), lambda i, j: (0, 0)),
        ],
        out_specs=pl.BlockSpec((tm, D_MODEL), lambda i, j: (i, 0)),
        out_shape=jax.ShapeDtypeStruct((T, D_MODEL), F32),
        scratch_shapes=[pltpu.VMEM((tm, D_MODEL), BF16), pltpu.VMEM((tm, D_MODEL), F32)],
        compiler_params=_cparams(("parallel", "arbitrary")),
        name="ffn_half_step",
    )(h, g, wg, wu, wd, gf)


def _proj_kernel(h_ref, g_ref, w_ref, f_ref, gq_ref):
    u = _rms(h_ref[...], g_ref[...]).astype(BF16)
    f_ref[...] = _dot(u, w_ref[:, :F_COLS])
    step = 512
    for c0 in range(0, G_COLS, step):
        c1 = min(c0 + step, G_COLS)
        gq_ref[:, c0:c1] = _dot(u, w_ref[:, F_COLS + c0:F_COLS + c1]).astype(BF16)


def _proj(h, g, wp):
    T = h.shape[0]
    tm = 512
    return pl.pallas_call(
        _proj_kernel,
        grid=(T // tm,),
        in_specs=[
            pl.BlockSpec((tm, D_MODEL), lambda i: (i, 0)),
            pl.BlockSpec((1, D_MODEL), lambda i: (0, 0)),
            pl.BlockSpec((D_MODEL, F_COLS + G_COLS), lambda i: (0, 0)),
        ],
        out_specs=[
            pl.BlockSpec((tm, F_COLS), lambda i: (i, 0)),
            pl.BlockSpec((tm, G_COLS), lambda i: (i, 0)),
        ],
        out_shape=[jax.ShapeDtypeStruct((T, F_COLS), F32),
                   jax.ShapeDtypeStruct((T, G_COLS), BF16)],
        compiler_params=_cparams(("parallel",)),
        name="in_proj",
    )(h, g, wp)


def _proj_weight(w_in):
    ncol = F_COLS + G_COLS
    src = np.zeros((ncol,), np.int32)
    scale = np.zeros((ncol,), np.float32)

    def put(dst, srcs, s=1.0):
        srcs = np.asarray(srcs)
        src[dst:dst + len(srcs)] = srcs
        scale[dst:dst + len(srcs)] = s

    half = MLA_ROPE // 2
    put(0, O_CQ + np.arange(MLA_Q_RANK))
    put(FB_CKV * LANE, O_CKV + np.arange(MLA_KV_RANK))
    put(FB_KROPE * LANE + MLA_NOPE, O_KROPE + np.arange(MLA_ROPE))
    put(FB_KSWAP * LANE + MLA_NOPE, O_KROPE + np.concatenate([np.arange(half, MLA_ROPE), np.arange(half)]))
    put(FB_NKC * LANE, O_NKC + np.arange(LANE))
    put(FB_NVC * LANE, O_NVC + np.arange(LANE))
    put(FB_NGATE * LANE, O_NGATE + np.arange(3 * NSA_HEADS))
    g0 = F_COLS
    qscale = 0.125 * LOG2E
    pair_perm = np.concatenate([np.concatenate([(j) * HALF + np.arange(HALF), (4 + j) * HALF + np.arange(HALF)])
                                for j in range(4)])
    put(g0 + GB_SQ * LANE, O_SQ + pair_perm, qscale)
    put(g0 + GB_NQ * LANE, O_NQ + pair_perm, qscale)
    put(g0 + GB_DQ * LANE, O_DQ + np.arange(512), qscale)
    put(g0 + GB_DK * LANE, O_DK + np.arange(512))
    put(g0 + GB_DV * LANE, O_DV + np.arange(512))
    for blk, off in ((GB_SK, O_SK), (GB_SV, O_SV), (GB_NKS, O_NKS), (GB_NVS, O_NVS),
                     (GB_NKW, O_NKW), (GB_NVW, O_NVW)):
        put(g0 + blk * LANE, off + np.arange(LANE))
    return (w_in[:, src] * scale[None, :]).astype(BF16)


PAIR_HEAD = np.array([[j, 4 + j] for j in range(4)])


def _mla_prep_kernel(cq_ref, ckv_ref, kr_ref, ks_ref, ct_ref, st_ref, qg_ref, kg_ref,
                     wa_ref, wb_ref, wk_ref, wv_ref, q_ref, k_ref, v_ref, *, scale):
    qn = _rms(cq_ref[0], qg_ref[...]).astype(BF16)
    kn = _rms(ckv_ref[0], kg_ref[...]).astype(BF16)
    ct = ct_ref[...]
    st = st_ref[...]
    krot = kr_ref[0] * ct + ks_ref[0] * st
    v_ref[0] = _dot(kn, wv_ref[...]).astype(BF16)
    for h in range(MLA_HEADS):
        sl = slice(h * LANE, (h + 1) * LANE)
        qa = _dot(qn, wa_ref[:, sl])
        qb = _dot(qn, wb_ref[:, sl])
        q_ref[0, :, sl] = ((qa * ct + qb * st) * scale).astype(BF16)
        k_ref[0, :, sl] = (_dot(kn, wk_ref[:, sl]) + krot).astype(BF16)


def _mla_prep(Fa, ctab, stab, qg, kg, wa, wb, wk, wv):
    B, S, _ = Fa.shape
    tm = 512
    HW = MLA_HEADS * LANE
    full = lambda shape: pl.BlockSpec(shape, lambda b, i: (0,) * len(shape))
    return pl.pallas_call(
        functools.partial(_mla_prep_kernel, scale=(MLA_NOPE + MLA_ROPE) ** -0.5 * LOG2E),
        grid=(B, S // tm),
        in_specs=[
            pl.BlockSpec((1, tm, MLA_Q_RANK), lambda b, i: (b, i, 0)),
            pl.BlockSpec((1, tm, LANE), lambda b, i: (b, i, FB_CKV)),
            pl.BlockSpec((1, tm, LANE), lambda b, i: (b, i, FB_KROPE)),
            pl.BlockSpec((1, tm, LANE), lambda b, i: (b, i, FB_KSWAP)),
            pl.BlockSpec((tm, LANE), lambda b, i: (i, 0)),
            pl.BlockSpec((tm, LANE), lambda b, i: (i, 0)),
            full((1, MLA_Q_RANK)), full((1, MLA_KV_RANK)),
            full((MLA_Q_RANK, HW)), full((MLA_Q_RANK, HW)),
            full((MLA_KV_RANK, HW)), full((MLA_KV_RANK, MLA_HEADS * MLA_V)),
        ],
        out_specs=[
            pl.BlockSpec((1, tm, HW), lambda b, i: (b, i, 0)),
            pl.BlockSpec((1, tm, HW), lambda b, i: (b, i, 0)),
            pl.BlockSpec((1, tm, MLA_HEADS * MLA_V), lambda b, i: (b, i, 0)),
        ],
        out_shape=[jax.ShapeDtypeStruct((B, S, HW), BF16),
                   jax.ShapeDtypeStruct((B, S, HW), BF16),
                   jax.ShapeDtypeStruct((B, S, MLA_HEADS * MLA_V), BF16)],
        compiler_params=_cparams(("parallel", "parallel")),
        name="mla_prep",
    )(Fa, Fa, Fa, Fa, ctab, stab, qg, kg, wa, wb, wk, wv)


def _mla_weights(w_uq, w_ukv):
    dq = MLA_NOPE + MLA_ROPE
    half = MLA_ROPE // 2
    z = lambda n: jnp.zeros((w_uq.shape[0], n), F32)
    wa, wb, wk, wv = [], [], [], []
    for h in range(MLA_HEADS):
        nope = w_uq[:, h * dq:h * dq + MLA_NOPE]
        r = w_uq[:, h * dq + MLA_NOPE:(h + 1) * dq]
        wa += [nope, r, z(LANE - dq)]
        wb += [z(MLA_NOPE), r[:, half:], r[:, :half], z(LANE - dq)]
        wk += [w_ukv[:, h * LANE:h * LANE + MLA_NOPE], jnp.zeros((w_ukv.shape[0], LANE - MLA_NOPE), F32)]
        wv += [w_ukv[:, h * LANE + MLA_NOPE:(h + 1) * LANE]]
    cat = lambda xs: jnp.concatenate(xs, axis=1).astype(BF16)
    return cat(wa), cat(wb), cat(wk), cat(wv)


def _rope_tables(S):
    half = MLA_ROPE // 2
    freqs = ROPE_THETA ** (-jnp.arange(half, dtype=F32) / half)
    ang = jnp.arange(S, dtype=jnp.int32)[:, None].astype(F32) * freqs
    cos, sin = jnp.cos(ang), jnp.sin(ang)
    ones = jnp.ones((S, MLA_NOPE), F32)
    zeros = jnp.zeros((S, LANE - MLA_NOPE - MLA_ROPE), F32)
    ctab = jnp.concatenate([ones, cos, cos, zeros], 1)
    stab = jnp.concatenate([jnp.zeros((S, MLA_NOPE), F32), -sin, sin, zeros], 1)
    return ctab, stab


def _t5_bucket(dist):
    n = jnp.maximum(dist, 0)
    max_exact = N_BUCKETS // 2
    nf = jnp.maximum(n, 1).astype(F32)
    large = max_exact + (jnp.log(nf / max_exact) / math.log(MAX_DIST / max_exact)
                         * (N_BUCKETS - max_exact)).astype(jnp.int32)
    large = jnp.minimum(large, N_BUCKETS - 1)
    return jnp.where(n < max_exact, n, large)


def _bias_tiles(tab, t):
    assert t >= MAX_DIST
    i = jnp.arange(t, dtype=jnp.int32)[:, None]
    j = jnp.arange(t, dtype=jnp.int32)[None, :]
    tiles = []
    tab_t = tab.T[:, :, None, None]
    for rel in range(2):
        b = _t5_bucket(rel * t + j - i)[None]
        tile = jnp.zeros((tab.shape[1], t, t), F32)
        for n in range(N_BUCKETS):
            tile = jnp.where(b == n, tab_t[:, n], tile)
        tiles.append(tile)
    far = tab[N_BUCKETS - 1][:, None, None, None]
    return (jnp.stack(tiles, 1) - far) * LOG2E


def _flash_kernel(*refs, t, kc, window, shared_qk, has_bias, bias_shared, bias_resident, has_sel, has_sink, diff,
                  out_dtype):
    qt_ref, kt_ref, first_ref, last_ref, rel_ref = refs[:5]
    refs = list(refs[5:])
    if shared_qk:
        q_ref, k_ref = refs[:2]
        refs = refs[2:]
    else:
        qa_ref, qb_ref, ka_ref, kb_ref = refs[:4]
        refs = refs[4:]
    vt_ref = refs.pop(0)
    if has_bias:
        ba_ref = refs.pop(0)
        bb_ref = ba_ref if bias_shared else refs.pop(0)
    if has_sel:
        sa_ref, sb_ref, et_ref = refs[:3]
        refs = refs[3:]
    if has_sink:
        sink_ref = refs.pop(0)
    if diff:
        lam_ref, sub_ref = refs[:2]
        refs = refs[2:]
    o_ref, m_scr, l_scr, acc_scr, p_scr = refs

    step = pl.program_id(2)
    qt = qt_ref[step]
    kt = kt_ref[step]
    lo = lax.broadcasted_iota(jnp.int32, (1, LANE), 1) < HALF

    @pl.when(first_ref[step] == 1)
    def _():
        for a in range(2):
            if has_sink:
                m_scr[a] = jnp.broadcast_to(sink_ref[0, a:a + 1, 0:1], (1, t))
                l_scr[a] = jnp.ones((1, t), F32)
            else:
                m_scr[a] = jnp.full((1, t), NEG, F32)
                l_scr[a] = jnp.zeros((1, t), F32)
            acc_scr[a] = jnp.zeros((LANE, t), F32)

    if shared_qk:
        q = q_ref[0]
        zero = jnp.zeros_like(q)
        qs = (jnp.where(lo, q, zero), jnp.where(lo, zero, q))
        ks = (k_ref[0], k_ref[0])
    else:
        qs = (qa_ref[0], qb_ref[0])
        ks = (ka_ref[0], kb_ref[0])
    vt = vt_ref[0]

    def process(rel):
        col_slices = [slice(c, c + LANE) for c in range(0, t, LANE)]

        def unit_range(kh, c):
            base = rel * t + c - kh
            return base, base - (kc - 1), base + (LANE - 1)

        def unit_dead(kh, c):
            _, dmin, dmax = unit_range(kh, c)
            return (rel == 0 and dmax < 0) or (window is not None and dmin >= window)

        jobs = [(kh, a) for kh in range(0, t, kc) for a in range(2)
                if not all(unit_dead(kh, cs.start) for cs in col_slices)]

        def scores(job):
            kh, a = job
            kk, qq = ks[a][kh:kh + kc], qs[a]
            if has_sel:
                kk = jnp.concatenate([kk, et_ref[kh:kh + kc, :]], axis=1)
                qq = jnp.concatenate([qq, (sa_ref if a == 0 else sb_ref)[0]], axis=1)
            return _dot_nt(kk, qq)

        m = [[m_scr[a, :, cs] for cs in col_slices] for a in range(2)]
        l = [[l_scr[a, :, cs] for cs in col_slices] for a in range(2)]
        acc = [acc_scr[0], acc_scr[1]]
        nxt = scores(jobs[0])
        for ji, (kh, a) in enumerate(jobs):
            keys = slice(kh, kh + kc)
            st = nxt
            if ji + 1 < len(jobs):
                nxt = scores(jobs[ji + 1])
            b_ref = (ba_ref if a == 0 else bb_ref) if has_bias else None
            alphas = []
            for ci, cols in enumerate(col_slices):
                base, dmin, dmax = unit_range(kh, cols.start)
                if unit_dead(kh, cols.start):
                    alphas.append(jnp.ones((1, LANE), F32))
                    p_scr[a, keys, cols] = jnp.zeros((kc, LANE), BF16)
                    continue
                s = st[:, cols]
                if has_bias and rel < 2 and dmin < MAX_DIST:
                    s = s + b_ref[0, 0, rel if bias_resident else 0, keys, cols]
                need_lo = rel == 0 and dmin < 0
                need_hi = window is not None and dmax >= window
                if need_lo or need_hi:
                    d = (lax.broadcasted_iota(jnp.int32, (kc, LANE), 1)
                         - lax.broadcasted_iota(jnp.int32, (kc, LANE), 0)) + base
                    valid = (d >= 0) if need_lo else None
                    if need_hi:
                        valid = (d < window) if valid is None else (valid & (d < window))
                    s = jnp.where(valid, s, NEG)
                m_prev = m[a][ci]
                m_new = jnp.maximum(m_prev, jnp.max(s, 0, keepdims=True))
                alpha = jnp.exp2(m_prev - m_new)
                p = jnp.exp2(s - m_new)
                l[a][ci] = alpha * l[a][ci] + jnp.sum(p, 0, keepdims=True)
                m[a][ci] = m_new
                alphas.append(alpha)
                p_scr[a, keys, cols] = p.astype(BF16)
            acc[a] = jnp.concatenate(alphas, axis=1) * acc[a] + _dot(vt[:, keys], p_scr[a, keys, :])
        for a in range(2):
            for ci, cs in enumerate(col_slices):
                m_scr[a, :, cs] = m[a][ci]
                l_scr[a, :, cs] = l[a][ci]
            acc_scr[a] = acc[a]

    for r in range(2 if window is not None else 3):
        @pl.when(rel_ref[step] == r)
        def _(r=r):
            process(r)

    @pl.when(last_ref[step] == 1)
    def _():
        o0 = acc_scr[0] / l_scr[0]
        o1 = acc_scr[1] / l_scr[1]
        if diff:
            o = (o0 - lam_ref[0:1, 0:1] * o1).T
            o = _rms(o, sub_ref[...]) * lam_ref[1:2, 0:1]
        else:
            feat = lax.broadcasted_iota(jnp.int32, (LANE, 1), 0)
            o = jnp.where(feat < HALF, o0, o1).T
        o_ref[0] = o.astype(out_dtype)


def _flash(q_arr, k_arr, vt_arr, *, npairs, t, q_blk, k_blk, v_blk, window=None, shared_qk=True,
           bias=None, bias_shared=False, sel=None, emat=None, sinks=None, diff=None, out_dtype=BF16):
    B, S, _ = q_arr.shape
    nq = S // t
    nprev = None if window is None else -(-(window - 1) // t)
    assert window is None or window <= t + 1, "a window must end in the previous tile"
    steps = []
    for qi in range(nq):
        k0 = 0 if nprev is None else max(0, qi - nprev)
        for ki in range(k0, qi + 1):
            steps.append((qi, ki, int(ki == k0), int(ki == qi), min(qi - ki, 2)))
    tabs = [jnp.asarray(np.array([s[c] for s in steps], np.int32)) for c in range(5)]
    nsteps = len(steps)
    kc = min(t, KEY_CHUNK)
    bias_resident = t <= 512

    def qmap(off, stride):
        return lambda b, p, s, qt, kt, fr, la, rl: (b, qt[s], off + stride * p)

    def kmap(off, stride):
        return lambda b, p, s, qt, kt, fr, la, rl: (b, kt[s], off + stride * p)

    in_specs, args = [], []
    if shared_qk:
        in_specs += [pl.BlockSpec((1, t, LANE), qmap(*q_blk)), pl.BlockSpec((1, t, LANE), kmap(*k_blk))]
        args += [q_arr, k_arr]
    else:
        in_specs += [pl.BlockSpec((1, t, LANE), qmap(*q_blk[0])), pl.BlockSpec((1, t, LANE), qmap(*q_blk[1])),
                     pl.BlockSpec((1, t, LANE), kmap(*k_blk[0])), pl.BlockSpec((1, t, LANE), kmap(*k_blk[1]))]
        args += [q_arr, q_arr, k_arr, k_arr]
    voff, vstride = v_blk
    in_specs.append(pl.BlockSpec((1, LANE, t),
                                 lambda b, p, s, qt, kt, fr, la, rl: (b, voff + vstride * p, kt[s])))
    args.append(vt_arr)
    if bias is not None:
        for a in range(1 if bias_shared else 2):
            if bias_resident:
                in_specs.append(pl.BlockSpec((1, 1, 2, t, t),
                                             lambda b, p, s, qt, kt, fr, la, rl, a=a: (a, p, 0, 0, 0)))
            else:
                in_specs.append(pl.BlockSpec(
                    (1, 1, 1, t, t), lambda b, p, s, qt, kt, fr, la, rl, a=a: (a, p, jnp.minimum(rl[s], 1), 0, 0)))
            args.append(bias)
    if sel is not None:
        nselp = emat.shape[1]
        for a in range(2):
            in_specs.append(pl.BlockSpec((1, t, nselp),
                                         lambda b, p, s, qt, kt, fr, la, rl, a=a: (b, qt[s], a)))
            args.append(sel)
        in_specs.append(pl.BlockSpec((t, nselp), lambda b, p, s, qt, kt, fr, la, rl: (kt[s], 0)))
        args.append(emat)
    if sinks is not None:
        in_specs.append(pl.BlockSpec((1, 8, LANE), lambda b, p, s, qt, kt, fr, la, rl: (p, 0, 0)))
        args.append(sinks)
    if diff is not None:
        lam_arr, sub = diff
        in_specs.append(pl.BlockSpec((8, LANE), lambda b, p, s, qt, kt, fr, la, rl: (0, 0)))
        in_specs.append(pl.BlockSpec((1, LANE), lambda b, p, s, qt, kt, fr, la, rl: (0, 0)))
        args += [lam_arr, sub]

    kern = functools.partial(
        _flash_kernel, t=t, kc=kc, window=window, shared_qk=shared_qk, has_bias=bias is not None,
        bias_shared=bias_shared, bias_resident=bias_resident, has_sel=sel is not None, has_sink=sinks is not None,
        diff=diff is not None, out_dtype=out_dtype)
    return pl.pallas_call(
        kern,
        grid_spec=pltpu.PrefetchScalarGridSpec(
            num_scalar_prefetch=5,
            grid=(B, npairs, nsteps),
            in_specs=in_specs,
            out_specs=pl.BlockSpec((1, t, LANE), lambda b, p, s, qt, kt, fr, la, rl: (b, qt[s], p)),
            scratch_shapes=[pltpu.VMEM((2, 1, t), F32), pltpu.VMEM((2, 1, t), F32),
                            pltpu.VMEM((2, LANE, t), F32), pltpu.VMEM((2, t, t), BF16)],
        ),
        out_shape=jax.ShapeDtypeStruct((B, S, npairs * LANE), out_dtype),
        compiler_params=_cparams(("parallel", "parallel", "arbitrary")),
        name="flash_pairs",
    )(*tabs, *args)


def _compress_kernel(r_ref, pt_ref, pb_ref, w1_ref, w2_ref, o_ref):
    r = r_ref[0, 0]
    n = r.shape[0]
    xt = (r + pt_ref[0]).astype(BF16)
    xb = (r + pb_ref[0]).astype(BF16)
    out = jnp.zeros((n, LANE), F32)
    for h in range(2):
        top = _dot(xt, w1_ref[0, h, 0])
        bot = _dot(xb, w1_ref[0, h, 1])
        x = top + pltpu.roll(bot, n - 1, 0)
        hd = 0.5 * x * (1.0 + jnp.tanh(math.sqrt(2.0 / math.pi) * (x + 0.044715 * (x * x * x))))
        out = out + _dot(hd.astype(BF16), w2_ref[0, h])
    o_ref[0, 0] = out.astype(BF16)


def _compress(r2, ptop, pbot, w1e, w2e):
    _, B, n, K = r2.shape
    return pl.pallas_call(
        _compress_kernel,
        grid=(2, B),
        in_specs=[
            pl.BlockSpec((1, 1, n, K), lambda c, b: (c, b, 0, 0)),
            pl.BlockSpec((1, 1, K), lambda c, b: (c, 0, 0)),
            pl.BlockSpec((1, 1, K), lambda c, b: (c, 0, 0)),
            pl.BlockSpec((1, 2, 2, K, NSA_CMP_HIDDEN), lambda c, b: (c, 0, 0, 0, 0)),
            pl.BlockSpec((1, 2, NSA_CMP_HIDDEN, LANE), lambda c, b: (c, 0, 0, 0)),
        ],
        out_specs=pl.BlockSpec((1, 1, n, LANE), lambda c, b: (c, b, 0, 0)),
        out_shape=jax.ShapeDtypeStruct((2, B, n, LANE), BF16),
        compiler_params=_cparams(("parallel", "parallel")),
        name="nsa_compress",
    )(r2, ptop, pbot, w1e, w2e)


def _compress_weights(cmp_pos, cmp_w1, cmp_w2):
    L2 = NSA_CMP_STRIDE
    d = NSA_HD
    w1 = cmp_w1.reshape(2, 2, L2, d, NSA_CMP_HIDDEN)
    z = jnp.zeros_like(w1)
    e0 = jnp.concatenate([w1, z], axis=3)
    e1 = jnp.concatenate([z, w1], axis=3)
    w1e = jnp.stack([e0, e1], axis=1).reshape(2, 2, 2, L2 * 2 * d, NSA_CMP_HIDDEN).astype(BF16)
    pos = cmp_pos.reshape(2, 2, L2, 1, d)
    pos = jnp.broadcast_to(pos, (2, 2, L2, 2, d)).reshape(2, 2, 1, L2 * 2 * d)
    zw = jnp.zeros_like(cmp_w2)
    w2e = jnp.stack([jnp.concatenate([cmp_w2, zw], -1), jnp.concatenate([zw, cmp_w2], -1)], 1).astype(BF16)
    return pos[:, 0], pos[:, 1], w1e, w2e


def _cmp_attn_kernel(q_ref, kc_ref, vc_ref, ov_ref, oc_ref, sel_ref, *, tq, ncp, nselp):
    i = pl.program_id(1)
    qpos = i * tq + lax.broadcasted_iota(jnp.int32, (tq, 1), 0)
    cidx = lax.broadcasted_iota(jnp.int32, (1, ncp), 1)
    cvalid = (cidx * NSA_CMP_STRIDE + (NSA_CMP_LEN - 1)) <= qpos
    lane = lax.broadcasted_iota(jnp.int32, (1, LANE), 1)
    lo = lane < HALF
    kc = kc_ref[0, 0]
    vc = vc_ref[0, 0]
    psum = [jnp.zeros((tq, ncp), F32), jnp.zeros((tq, ncp), F32)]
    for j in range(4):
        q2 = q_ref[0, :, j * LANE:(j + 1) * LANE]
        zero = jnp.zeros_like(q2)
        outs = []
        for a in range(2):
            qa = jnp.where(lo, q2, zero) if a == 0 else jnp.where(lo, zero, q2)
            s = jnp.where(cvalid, _dot_nt(qa, kc), NEG)
            m = jnp.max(s, -1, keepdims=True)
            e = jnp.where(cvalid, jnp.exp2(s - m), 0.0)
            den = jnp.sum(e, -1, keepdims=True)
            p = e / jnp.where(den > 0, den, 1.0)
            psum[a] = psum[a] + p
            outs.append(_dot(p.astype(BF16), vc))
        oc_ref[0, :, j * LANE:(j + 1) * LANE] = jnp.where(lo, outs[0], outs[1])

    n = lax.broadcasted_iota(jnp.int32, (1, nselp), 1)
    cur = qpos >> int(math.log2(NSA_SEL_LEN))
    causal = (n * NSA_SEL_LEN) <= qpos
    forced = (n == 0) | (n == cur) | (n == cur - 1)
    nf = n.astype(F32)
    for a in range(2):
        imp = _dot_exact(psum[a], ov_ref[...])
        x = jnp.where(causal, jnp.where(forced, NSA_FORCE, imp), -1.0)
        picked = jnp.zeros((tq, nselp), F32)
        for _ in range(NSA_TOPK):
            mx = jnp.max(x, -1, keepdims=True)
            idx = jnp.min(jnp.where(x == mx, nf, float(nselp)), -1, keepdims=True)
            hit = nf == idx
            picked = jnp.where(hit, 1.0, picked)
            x = jnp.where(hit, -2.0, x)
        sel_ref[0, :, a * nselp:(a + 1) * nselp] = jnp.where(causal & (picked > 0.5), 0.0, NEG).astype(BF16)


def _cmp_attn(G3, kv2, ovl):
    B, S, _ = G3.shape
    ncp = kv2.shape[2]
    nselp = ovl.shape[1]
    tq = 512
    return pl.pallas_call(
        functools.partial(_cmp_attn_kernel, tq=tq, ncp=ncp, nselp=nselp),
        grid=(B, S // tq),
        in_specs=[
            pl.BlockSpec((1, tq, 4 * LANE), lambda b, i: (b, i, GB_NQ // 4)),
            pl.BlockSpec((1, 1, ncp, LANE), lambda b, i: (0, b, 0, 0)),
            pl.BlockSpec((1, 1, ncp, LANE), lambda b, i: (1, b, 0, 0)),
            pl.BlockSpec((ncp, nselp), lambda b, i: (0, 0)),
        ],
        out_specs=[
            pl.BlockSpec((1, tq, 4 * LANE), lambda b, i: (b, i, 0)),
            pl.BlockSpec((1, tq, 2 * nselp), lambda b, i: (b, i, 0)),
        ],
        out_shape=[jax.ShapeDtypeStruct((B, S, 4 * LANE), F32),
                   jax.ShapeDtypeStruct((B, S, 2 * nselp), BF16)],
        compiler_params=_cparams(("parallel", "parallel")),
        name="nsa_cmp_attn_topk",
    )(G3, kv2, kv2, ovl)


def _merge_kernel(h_ref, g_ref, ya_ref, yb_ref, oc_ref, os_ref, ow_ref, ng_ref, yd_ref,
                  wg_ref, wb_ref, wo_ref, p3_ref, o_ref):
    h = h_ref[...]
    u = _rms(h, g_ref[...]).astype(BF16)
    gexp = _dot_exact(jax.nn.sigmoid(ng_ref[...]), p3_ref[...])
    W = 4 * LANE
    yc = gexp[:, :W] * oc_ref[...] + gexp[:, W:2 * W] * os_ref[...] + gexp[:, 2 * W:] * ow_ref[...]
    ys = (ya_ref[...], yb_ref[...], yc.astype(BF16), yd_ref[...])
    merged = jnp.zeros(h.shape, F32)
    for i in range(4):
        gate = jax.nn.sigmoid(_dot(u, wg_ref[i]))
        merged = merged + gate * _dot(ys[i], wb_ref[i])
    o_ref[...] = h + _dot(merged.astype(BF16), wo_ref[...])


def _merge(h, g, ya, yb, oc, osel, ow, Fa, yd, wg, wb, wo, p3):
    T = h.shape[0]
    tm = 256
    W = 4 * LANE
    row = lambda w: pl.BlockSpec((tm, w), lambda i: (i, 0))
    const = lambda shape: pl.BlockSpec(shape, lambda i: (0,) * len(shape))
    return pl.pallas_call(
        _merge_kernel,
        grid=(T // tm,),
        in_specs=[
            row(D_MODEL), const((1, D_MODEL)), row(W), row(W), row(W), row(W), row(W),
            pl.BlockSpec((tm, LANE), lambda i: (i, FB_NGATE)), row(W),
            const((4, D_MODEL, D_MODEL)), const((4, W, D_MODEL)), const((D_MODEL, D_MODEL)),
            const((LANE, 3 * W)),
        ],
        out_specs=row(D_MODEL),
        out_shape=jax.ShapeDtypeStruct((T, D_MODEL), F32),
        compiler_params=_cparams(("parallel",)),
        name="gated_merge",
    )(h, g, ya, yb, oc, osel, ow, Fa, yd, wg, wb, wo, p3)


def _gate_expand_matrix():
    W = 4 * LANE
    m = np.zeros((LANE, 3 * W), np.float32)
    for j in range(4):
        for a in range(2):
            head = PAIR_HEAD[j, a]
            for r in range(3):
                c0 = r * W + j * LANE + a * HALF
                m[head * 3 + r, c0:c0 + HALF] = 1.0
    return jnp.asarray(m, BF16)


def _pair_rows(w):
    perm = np.concatenate([np.arange(HALF) + PAIR_HEAD[j, a] * HALF for j in range(4) for a in range(2)])
    return w[perm]


def kernel(x, norm_g, w_in, mla_q_norm, mla_kv_norm, mla_w_uq, mla_w_ukv, swa_sinks,
           nsa_cmp_pos, nsa_cmp_w1, nsa_cmp_w2, diff_lambda, diff_subln, rel_bias_table,
           w_branch, w_gate, w_o, ffn_w_gate, ffn_w_up, ffn_w_down, final_g):
    B, S, D = x.shape
    T = B * S
    depth = w_in.shape[0]
    t_big = min(S, 1024)
    t_att = 512
    ncp = S // NSA_CMP_STRIDE
    nsel = S // NSA_SEL_LEN
    nselp = -(-nsel // LANE) * LANE

    tab_swa = rel_bias_table[:, :SWA_HEADS]
    tab_nsa = rel_bias_table[:, SWA_HEADS:SWA_HEADS + NSA_HEADS]
    tab_diff = rel_bias_table[:, SWA_HEADS + NSA_HEADS:]
    pair_heads = PAIR_HEAD.T.reshape(-1)
    bias_swa = _bias_tiles(tab_swa[:, pair_heads], t_att).reshape(2, 4, 2, t_att, t_att)
    bias_win = _bias_tiles(tab_nsa[:, pair_heads], t_att).reshape(2, 4, 2, t_att, t_att)
    bias_sel = _bias_tiles(tab_nsa[:, pair_heads], t_big).reshape(2, 4, 2, t_big, t_big)
    bias_diff = _bias_tiles(tab_diff, t_big)[None]
    ctab, stab = _rope_tables(S)

    ci = np.arange(ncp)[:, None]
    ni = np.arange(nselp)[None, :]
    nc_real = (S - NSA_CMP_LEN) // NSA_CMP_STRIDE + 1
    ovl = ((ci * NSA_CMP_STRIDE < ni * NSA_SEL_LEN + NSA_SEL_LEN)
           & (ni * NSA_SEL_LEN < ci * NSA_CMP_STRIDE + NSA_CMP_LEN) & (ci < nc_real) & (ni < nsel))
    ovl = jnp.asarray(ovl.astype(np.float32), BF16)
    emat = jnp.asarray((np.arange(S)[:, None] // NSA_SEL_LEN == np.arange(nselp)[None, :]).astype(np.float32), BF16)
    p3 = _gate_expand_matrix()

    h = x.reshape(T, D)
    for l in range(depth):
        gl = norm_g[l].reshape(3, 1, D)
        h = _ffn(h, gl[0], ffn_w_gate[l, 0].astype(BF16), ffn_w_up[l, 0].astype(BF16),
                 ffn_w_down[l, 0].astype(BF16), gl[0], final=False)

        Fa, Ga = _proj(h, gl[1], _proj_weight(w_in[l]))
        Fa = Fa.reshape(B, S, F_COLS)
        Ga = Ga.reshape(B, S, G_COLS)

        wa, wb_, wk, wv = _mla_weights(mla_w_uq[l], mla_w_ukv[l])
        Q, K, V = _mla_prep(Fa, ctab, stab, mla_q_norm[l][None], mla_kv_norm[l][None], wa, wb_, wk, wv)
        ya = _flash(Q, K, V.transpose(0, 2, 1), npairs=4, t=t_big, shared_qk=False,
                    q_blk=((0, 2), (1, 2)), k_blk=((0, 2), (1, 2)), v_blk=(0, 1))

        sink = ((swa_sinks[l] - tab_swa[N_BUCKETS - 1]) * LOG2E)[PAIR_HEAD]
        sink = jnp.broadcast_to(jnp.pad(sink, ((0, 0), (0, 6)))[:, :, None], (4, 8, LANE)).astype(F32)
        GvT = Ga[:, :, GB_DV * LANE:].transpose(0, 2, 1)
        yb = _flash(Ga, Ga, GvT, npairs=4, t=t_att, window=SWA_WINDOW,
                    q_blk=(GB_SQ, 1), k_blk=(GB_SK, 0), v_blk=(GB_SV - GB_DV, 0), bias=bias_swa, sinks=sink)

        r2 = jnp.stack([Fa[:, :, FB_NKC * LANE:(FB_NKC + 1) * LANE],
                        Fa[:, :, FB_NVC * LANE:(FB_NVC + 1) * LANE]]).reshape(2, B, ncp, NSA_CMP_STRIDE * LANE)
        ptop, pbot, w1e, w2e = _compress_weights(nsa_cmp_pos[l], nsa_cmp_w1[l], nsa_cmp_w2[l])
        kv2 = _compress(r2, ptop, pbot, w1e, w2e)
        oc, sel = _cmp_attn(Ga, kv2, ovl)
        osel = _flash(Ga, Ga, GvT, npairs=4, t=t_big, q_blk=(GB_NQ, 1), k_blk=(GB_NKS, 0),
                      v_blk=(GB_NVS - GB_DV, 0), bias=bias_sel, sel=sel, emat=emat, out_dtype=F32)
        ow = _flash(Ga, Ga, GvT, npairs=4, t=t_att, window=NSA_WINDOW, q_blk=(GB_NQ, 1), k_blk=(GB_NKW, 0),
                    v_blk=(GB_NVW - GB_DV, 0), bias=bias_win, out_dtype=F32)

        lam_init = 0.8 - 0.6 * math.exp(-0.3 * l)
        lp = diff_lambda[l].astype(F32)
        lam = jnp.exp(jnp.sum(lp[0] * lp[1])) - jnp.exp(jnp.sum(lp[2] * lp[3])) + lam_init
        lam_arr = jnp.concatenate([jnp.full((1, LANE), 1.0, F32) * lam,
                                   jnp.full((7, LANE), 1.0 - lam_init, F32)], 0)
        yd = _flash(Ga, Ga, GvT, npairs=DIFF_HEADS, t=t_big, q_blk=(GB_DQ, 1), k_blk=(GB_DK, 1), v_blk=(0, 1),
                    bias=bias_diff, bias_shared=True, diff=(lam_arr, diff_subln[l][None]))

        wbr = jnp.stack([w_branch[l, 0], _pair_rows(w_branch[l, 1]), _pair_rows(w_branch[l, 2]),
                         w_branch[l, 3]]).astype(BF16)
        f2 = lambda a: a.reshape(T, a.shape[-1])
        h = _merge(h, gl[1], f2(ya), f2(yb), f2(oc), f2(osel), f2(ow), f2(Fa), f2(yd),
                   w_gate[l].astype(BF16), wbr, w_o[l].astype(BF16), p3)

        h = _ffn(h, gl[2], ffn_w_gate[l, 1].astype(BF16), ffn_w_up[l, 1].astype(BF16),
                 ffn_w_down[l, 1].astype(BF16), final_g[None], final=(l == depth - 1))
    return h.reshape(B, S, D)
```
